```python
import jax
import jax.numpy as jnp
from jax import lax
import numpy as np

D_MODEL = 1024
BATCH = 32
SEQ = 2048
DEPTH = 4

CTX_LEN = 256
GRID_W = 64
N_DIR = 2
NORM_EPS = 1e-6

HG_HEADS = 4
HG_HEAD_K = 128
HG_HEAD_V = 128
HG_WIDTH = HG_HEADS * HG_HEAD_V
HG_CHUNK = 32
HG_COLS = 5 * HG_WIDTH

RW_HEADS = 8
RW_HEAD = 64
RW_WIDTH = RW_HEADS * RW_HEAD
RW_DECAY_LORA = 64
RW_AAA_LORA = 64
RW_GATE_LORA = 128
RW_GN_EPS = 64e-5
RW_SPLITS = [RW_WIDTH, 2 * RW_WIDTH, 3 * RW_WIDTH,
             3 * RW_WIDTH + N_DIR * RW_DECAY_LORA,
             3 * RW_WIDTH + N_DIR * (RW_DECAY_LORA + RW_AAA_LORA)]
RW_COLS = RW_SPLITS[-1] + RW_GATE_LORA

IN_COLS = HG_COLS + RW_COLS + 2 * D_MODEL
FFN_HIDDEN = -((-8 * D_MODEL) // (3 * 256)) * 256

kernel_name = "hgrn2_rwkv7_gated_hybrid_dit"


def rms_norm(x, g, eps=NORM_EPS):
    xf = x.astype(jnp.float32)
    y = xf * lax.rsqrt(jnp.mean(xf * xf, axis=-1, keepdims=True) + eps)
    return (y * g.astype(jnp.float32)).astype(x.dtype)


def modulate(h, shift, scale):
    return h * (1 + scale) + shift


def adaln_mod(cond, w, b):
    m = (jax.nn.silu(cond) @ w + b)[..., None, :]
    return jnp.split(m, 6, axis=-1)


def grid_shift(p):
    B, T, C = p.shape
    rows = T // GRID_W
    g = p.reshape(B, rows, GRID_W, C // 4, 4)
    left = jnp.pad(g[:, :, :-1, :, 0], ((0, 0), (0, 0), (1, 0), (0, 0)))
    right = jnp.pad(g[:, :, 1:, :, 1], ((0, 0), (0, 0), (0, 1), (0, 0)))
    up = jnp.pad(g[:, :-1, :, :, 2], ((0, 0), (1, 0), (0, 0), (0, 0)))
    down = jnp.pad(g[:, 1:, :, :, 3], ((0, 0), (0, 1), (0, 0), (0, 0)))
    return jnp.stack([left, right, up, down], axis=-1).reshape(B, T, C)


def seq_shift(p):
    B, L, C = p.shape
    g = p.reshape(B, L, C // 2, 2)
    prev = jnp.pad(g[:, :-1, :, 0], ((0, 0), (1, 0), (0, 0)))
    nxt = jnp.pad(g[:, 1:, :, 1], ((0, 0), (0, 1), (0, 0)))
    return jnp.stack([prev, nxt], axis=-1).reshape(B, L, C)


def gla_chunked(k, v, log_f, s0, q):
    T = k.shape[-2]
    n = T // HG_CHUNK

    def blocks(a):
        a = a.reshape(a.shape[:-2] + (n, HG_CHUNK, a.shape[-1]))
        return jnp.moveaxis(a, -3, 0)

    lower = jnp.tril(jnp.ones((HG_CHUNK, HG_CHUNK), dtype=bool))[:, :, None]

    def step(S, xs):
        kc, vc, gc = xs[0], xs[1], xs[2]
        b = jnp.cumsum(gc, axis=-2)
        b_end = b[..., -1:, :]
        S_new = (jnp.swapaxes(jnp.exp(b_end), -1, -2) * S
                 + jnp.einsum('...ck,...cv->...kv', kc * jnp.exp(b_end - b), vc))
        if q is None:
            return S_new, None
        qc = xs[3]
        dec = jnp.exp(jnp.where(lower, b[..., :, None, :] - b[..., None, :, :], -jnp.inf))
        scores = jnp.einsum('...ik,...jk,...ijk->...ij', qc, kc, dec)
        o = (jnp.einsum('...ij,...jv->...iv', scores, vc)
             + jnp.einsum('...ik,...kv->...iv', qc * jnp.exp(b), S))
        return S_new, o

    xs = (blocks(k), blocks(v), blocks(log_f))
    if q is not None:
        xs = xs + (blocks(q),)
    s_final, o = lax.scan(step, s0, xs)
    if q is None:
        return None, s_final
    o = jnp.moveaxis(o, 0, -3)
    return o.reshape(o.shape[:-3] + (T, o.shape[-1])), s_final


def hgrn2_branch(p, lb, norm_g, s0, read):
    B, T, _ = p.shape
    q, f, i, og = jnp.split(p, [HG_WIDTH, 3 * HG_WIDTH, 4 * HG_WIDTH], axis=-1)
    f = f.astype(jnp.float32).reshape(B, T, N_DIR, HG_WIDTH)
    log_f = jnp.logaddexp(jnp.log(lb), jnp.log1p(-lb) + jax.nn.log_sigmoid(f))
    k = (1.0 - lb) * jax.nn.sigmoid(-f)

    heads = lambda a: a.reshape(B, T, HG_HEADS, -1).transpose(0, 2, 1, 3)
    per_dir = lambda a: jnp.stack([heads(a[:, :, 0]), jnp.flip(heads(a[:, :, 1]), axis=2)])
    shared = lambda a: jnp.stack([a, jnp.flip(a, axis=2)])

    v = shared(heads(i.astype(jnp.float32)))
    qd = shared(heads(jax.nn.silu(q.astype(jnp.float32)) * HG_HEAD_K ** -0.5)) if read else None
    o, s_final = gla_chunked(per_dir(k), v, per_dir(log_f), s0, qd)
    if not read:
        return None, s_final
    o = rms_norm(o[0] + jnp.flip(o[1], axis=2), norm_g)
    o = o.transpose(0, 2, 1, 3).reshape(B, T, HG_WIDTH) * jax.nn.silu(og.astype(jnp.float32))
    return o.astype(p.dtype), s_final


def rwkv7_branch(p, shift, rw, s0, read):
    mu, w0, w2, a0, a2, g2, k_k, k_a, r_k, gn_g, gn_b = rw
    B, T, _ = p.shape
    pf = p.astype(jnp.float32)
    pf = pf + mu * (shift(pf) - pf)
    r, k, v, wd, ad, gd = jnp.split(pf, RW_SPLITS, axis=-1)
    w = -jax.nn.softplus(-(w0 + jnp.einsum('btpl,plc->btpc',
                                           jnp.tanh(wd.reshape(B, T, N_DIR, RW_DECAY_LORA)), w2))) - 0.5
    decay = jnp.exp(-jnp.exp(w))
    a = jax.nn.sigmoid(a0 + jnp.einsum('btpl,plc->btpc', ad.reshape(B, T, N_DIR, RW_AAA_LORA), a2))
    kk = (k * k_k).reshape(B, T, RW_HEADS, RW_HEAD)
    kk = (kk / jnp.maximum(jnp.linalg.norm(kk, axis=-1, keepdims=True), 1e-12)).reshape(B, T, RW_WIDTH)
    k_dir = k[:, :, None] * (1 + (a - 1) * k_a)
    b_dir = kk[:, :, None] * a

    def time_major(fwd, bwd):
        s = jnp.stack([fwd, jnp.flip(bwd, axis=1)])
        return s.reshape(N_DIR, B, T, RW_HEADS, RW_HEAD).transpose(2, 0, 1, 3, 4)

    per_dir = lambda t: time_major(t[:, :, 0], t[:, :, 1])
    shared = lambda t: time_major(t, t)
    xs = (per_dir(decay), per_dir(k_dir), per_dir(b_dir), shared(kk), shared(v))
    if read:
        xs = xs + (shared(r),)

    def step(S, xt):
        w_t, k_t, b_t, kk_t, v_t = xt[:5]
        sa = -jnp.einsum('...vk,...k->...v', S, kk_t)
        S = S * w_t[..., None, :] + sa[..., None] * b_t[..., None, :] + v_t[..., None] * k_t[..., None, :]
        if not read:
            return S, None
        return S, jnp.einsum('...vk,...k->...v', S, xt[5])

    s_final, ys = lax.scan(step, s0, xs)
    if not read:
        return None, s_final
    y = (ys[:, 0] + jnp.flip(ys[:, 1], axis=0)).transpose(1, 0, 2, 3)
    mean = jnp.mean(y, axis=-1, keepdims=True)
    var = jnp.mean(jnp.square(y - mean), axis=-1, keepdims=True)
    y = ((y - mean) * lax.rsqrt(var + RW_GN_EPS)).reshape(B, T, RW_WIDTH) * gn_g + gn_b
    hs = (B, T, RW_HEADS, RW_HEAD)
    bonus = jnp.sum(r.reshape(hs) * k_dir.sum(2).reshape(hs) * r_k, axis=-1, keepdims=True) * v.reshape(hs)
    g = jax.nn.sigmoid(gd) @ g2
    return ((y + bonus.reshape(B, T, RW_WIDTH)) * g).astype(p.dtype), s_final


def gated_merge(gates, o_hg, o_rw, proj_a, proj_b, w_out):
    g_hg, g_rw = jnp.split(gates, 2, axis=-1)
    m = jax.nn.sigmoid(g_hg) * (o_hg @ proj_a) + jax.nn.sigmoid(g_rw) * (o_rw @ proj_b)
    return m @ w_out


def ffn_sublayer(x, shift, scale, gate, g_pre, g_post, w1, w2):
    gt, up = jnp.split(modulate(rms_norm(x, g_pre), shift, scale) @ w1, 2, axis=-1)
    return x + gate * rms_norm((jax.nn.silu(gt) * up) @ w2, g_post)


def setup_inputs(seed: int = 0) -> dict:
    key = jax.random.key(seed)
    ks = jax.random.split(key, 26)
    D, L = D_MODEL, DEPTH
    nrm = lambda k, shape, s: jax.random.normal(k, shape, jnp.float32) * s
    return {
        "x": nrm(ks[0], (BATCH, SEQ, D), 1.0),
        "c": nrm(ks[1], (BATCH, D), 1.0),
        "ctx": nrm(ks[2], (BATCH, CTX_LEN, D), 1.0),
        "c_ctx": nrm(ks[3], (D,), 1.0),
        "ada_w": nrm(ks[4], (L, D, 6 * D), 0.5 * D ** -0.5),
        "ada_b": nrm(ks[5], (L, 6 * D), 0.02),
        "norm_g": 1.0 + nrm(ks[6], (L, 4, D), 0.05),
        "w_in": nrm(ks[7], (L, D, IN_COLS), D ** -0.5),
        "hg_lb_logits": 1.0 + nrm(ks[8], (L, N_DIR, HG_WIDTH), 0.5),
        "hg_norm_g": 1.0 + nrm(ks[9], (L, HG_HEAD_V), 0.05),
        "rw_mu": jax.random.uniform(ks[10], (L, RW_COLS), jnp.float32, 0.0, 1.0),
        "rw_w0": jax.random.uniform(ks[11], (L, N_DIR, RW_WIDTH), jnp.float32, -4.0, 0.0),
        "rw_w2": nrm(ks[12], (L, N_DIR, RW_DECAY_LORA, RW_WIDTH), 0.5 * RW_DECAY_LORA ** -0.5),
        "rw_a0": nrm(ks[13], (L, N_DIR, RW_WIDTH), 0.3),
        "rw_a2": nrm(ks[14], (L, N_DIR, RW_AAA_LORA, RW_WIDTH), 0.5 * RW_AAA_LORA ** -0.5),
        "rw_g2": nrm(ks[15], (L, RW_GATE_LORA, RW_WIDTH), RW_GATE_LORA ** -0.5),
        "rw_kk": 0.85 + nrm(ks[16], (L, RW_WIDTH), 0.05),
        "rw_ka": 1.0 + nrm(ks[17], (L, RW_WIDTH), 0.05),
        "rw_rk": nrm(ks[18], (L, RW_HEADS, RW_HEAD), 0.1),
        "rw_gn_g": 1.0 + nrm(ks[19], (L, RW_WIDTH), 0.05),
        "rw_gn_b": nrm(ks[20], (L, RW_WIDTH), 0.02),
        "proj_a": nrm(ks[21], (L, HG_WIDTH, D), HG_WIDTH ** -0.5),
        "proj_b": nrm(ks[22], (L, RW_WIDTH, D), RW_WIDTH ** -0.5),
        "w_out": nrm(ks[23], (L, D, D), D ** -0.5),
        "ffn_w1": nrm(ks[24], (L, D, 2 * FFN_HIDDEN), D ** -0.5),
        "ffn_w2": nrm(ks[25], (L, FFN_HIDDEN, D), FFN_HIDDEN ** -0.5),
    }


def reference(x, c, ctx, c_ctx, ada_w, ada_b, norm_g, w_in, hg_lb_logits, hg_norm_g,
              rw_mu, rw_w0, rw_w2, rw_a0, rw_a2, rw_g2, rw_kk, rw_ka, rw_rk, rw_gn_g, rw_gn_b,
              proj_a, proj_b, w_out, ffn_w1, ffn_w2):
    lb_cum = jnp.cumsum(jax.nn.softmax(hg_lb_logits.astype(jnp.float32), axis=0), axis=0)
    hg_lb = lb_cum - lb_cum[:1]
    B = x.shape[0]
    s_hg0 = jnp.zeros((N_DIR, B, HG_HEADS, HG_HEAD_K, HG_HEAD_V), jnp.float32)
    s_rw0 = jnp.zeros((N_DIR, B, RW_HEADS, RW_HEAD, RW_HEAD), jnp.float32)
    split_cols = lambda p: jnp.split(p, [HG_COLS, HG_COLS + RW_COLS], axis=-1)
    x_ctx = ctx
    for l in range(DEPTH):
        read_ctx = l < DEPTH - 1
        m_lat = adaln_mod(c, ada_w[l], ada_b[l])
        m_ctx = adaln_mod(c_ctx, ada_w[l], ada_b[l])
        rw = (rw_mu[l], rw_w0[l], rw_w2[l], rw_a0[l], rw_a2[l], rw_g2[l], rw_kk[l], rw_ka[l],
              rw_rk[l], rw_gn_g[l], rw_gn_b[l])

        hg_c, rw_c, gt_c = split_cols(modulate(rms_norm(x_ctx, norm_g[l, 0]), m_ctx[0], m_ctx[1]) @ w_in[l])
        hg_x, rw_x, gt_x = split_cols(modulate(rms_norm(x, norm_g[l, 0]), m_lat[0], m_lat[1]) @ w_in[l])

        o_hg_c, s_hg = hgrn2_branch(hg_c, hg_lb[l], hg_norm_g[l], s_hg0, read_ctx)
        o_hg_x, _ = hgrn2_branch(hg_x, hg_lb[l], hg_norm_g[l], s_hg, True)
        o_rw_c, s_rw = rwkv7_branch(rw_c, seq_shift, rw, s_rw0, read_ctx)
        o_rw_x, _ = rwkv7_branch(rw_x, grid_shift, rw, s_rw, True)

        y_x = gated_merge(gt_x, o_hg_x, o_rw_x, proj_a[l], proj_b[l], w_out[l])
        x = x + m_lat[2] * rms_norm(y_x, norm_g[l, 1])
        x = ffn_sublayer(x, m_lat[3], m_lat[4], m_lat[5], norm_g[l, 2], norm_g[l, 3], ffn_w1[l], ffn_w2[l])

        if read_ctx:
            y_c = gated_merge(gt_c, o_hg_c, o_rw_c, proj_a[l], proj_b[l], w_out[l])
            x_ctx = x_ctx + m_ctx[2] * rms_norm(y_c, norm_g[l, 1])
            x_ctx = ffn_sublayer(x_ctx, m_ctx[3], m_ctx[4], m_ctx[5], norm_g[l, 2], norm_g[l, 3],
                                 ffn_w1[l], ffn_w2[l])
    return x
```

```python
import functools

import numpy as np
import jax
import jax.numpy as jnp
from jax import lax
from jax.experimental import pallas as pl
from jax.experimental.pallas import tpu as pltpu

F32 = jnp.float32
BF16 = jnp.bfloat16

NORM_EPS = 1e-6
GRID_W = 64
HG_HEADS = 4
HG_HEAD = 128
HG_WIDTH = HG_HEADS * HG_HEAD
RW_HEADS = 8
RW_HEAD = 64
RW_WIDTH = RW_HEADS * RW_HEAD
RW_LORA = 64
RW_GATE_LORA = 128
RW_GN_EPS = 64e-5
RW_COLS = 3 * RW_WIDTH + 4 * RW_LORA + RW_GATE_LORA
HG_COLS = 5 * HG_WIDTH

TILE = 256
CHUNK = 64
LEVELS = 6
PAIR = 2 * RW_HEAD
VMEM_LIMIT = 56 * 1024 * 1024


def _bdot(a, b):
    return jnp.dot(a.astype(BF16), b.astype(BF16), preferred_element_type=F32)


def _bdot_nt(a, b):
    return lax.dot_general(a.astype(BF16), b.astype(BF16), (((1,), (1,)), ((), ())),
                           preferred_element_type=F32)


def _bdot_tn(a, b):
    return lax.dot_general(a.astype(BF16), b.astype(BF16), (((0,), (0,)), ((), ())),
                           preferred_element_type=F32)


def _split3(x):
    hi = x.astype(BF16)
    r1 = x - hi.astype(F32)
    mid = r1.astype(BF16)
    lo = (r1 - mid.astype(F32)).astype(BF16)
    return hi, mid, lo


def _dot_exact_lhs(m01, x):
    hi, mid, lo = _split3(x)
    d = lambda t: jnp.dot(m01, t, preferred_element_type=F32)
    return d(hi) + d(mid) + d(lo)


def _dot_exact_rhs(x, m01):
    hi, mid, lo = _split3(x)
    d = lambda t: jnp.dot(t, m01, preferred_element_type=F32)
    return d(hi) + d(mid) + d(lo)


def _rms(x, g):
    ms = jnp.mean(x * x, axis=-1, keepdims=True)
    return x * lax.rsqrt(ms + NORM_EPS) * g


def _sigmoid(x):
    return 1.0 / (1.0 + jnp.exp(-x))


def _softplus(x):
    return jnp.maximum(x, 0.0) + jnp.log1p(jnp.exp(-jnp.abs(x)))


def _params(*sem):
    return pltpu.CompilerParams(dimension_semantics=sem, vmem_limit_bytes=VMEM_LIMIT)


def _mods_kernel(c_ref, w_ref, b_ref, o_ref):
    c = c_ref[...]
    s = c * _sigmoid(c)
    sh, sm, sl = _split3(s)
    wh, wm, wl = _split3(w_ref[0])
    d = lambda a, b: jnp.dot(a, b, preferred_element_type=F32)
    acc = d(sh, wh) + d(sh, wm) + d(sm, wh) + d(sh, wl) + d(sl, wh) + d(sm, wm)
    o_ref[0] = acc + b_ref[0]


def _mods(cond, ada_w, ada_b):
    L, D, N = ada_w.shape
    rows = cond.shape[0]
    tn = 1536
    return pl.pallas_call(
        _mods_kernel,
        grid=(L, N // tn),
        in_specs=[pl.BlockSpec((rows, D), lambda l, n: (0, 0)),
                  pl.BlockSpec((1, D, tn), lambda l, n: (l, 0, n)),
                  pl.BlockSpec((1, 1, tn), lambda l, n: (l, 0, n))],
        out_specs=pl.BlockSpec((1, rows, tn), lambda l, n: (l, 0, n)),
        out_shape=jax.ShapeDtypeStruct((L, rows, N), F32),
        compiler_params=_params("arbitrary", "arbitrary"),
        name="adaln_mods",
    )(cond, ada_w, ada_b.reshape(L, 1, N))


def _inproj_kernel(x_ref, mod_ref, g_ref, whg_ref, wrw_ref, wgt_ref, ohg_ref, orw_ref, ogt_ref):
    D = x_ref.shape[-1]
    y = _rms(x_ref[0], g_ref[...])
    shift = mod_ref[0, :, 0:D]
    scale = mod_ref[0, :, D:2 * D]
    h = (y * (1.0 + scale) + shift).astype(BF16)
    ohg_ref[0] = jnp.dot(h, whg_ref[...], preferred_element_type=F32)
    orw_ref[0] = jnp.dot(h, wrw_ref[...], preferred_element_type=F32)
    ogt_ref[0] = jnp.dot(h, wgt_ref[...], preferred_element_type=F32)


def _inproj(xcat, modtab, g, whg, wrw, wgt):
    B, TC, D = xcat.shape
    nt = TC // TILE
    full = lambda a: pl.BlockSpec(a.shape, lambda b, t: (0,) * a.ndim)
    outs = [jax.ShapeDtypeStruct((B, TC, w.shape[1]), F32) for w in (whg, wrw, wgt)]
    return pl.pallas_call(
        _inproj_kernel,
        grid=(B, nt),
        in_specs=[pl.BlockSpec((1, TILE, D), lambda b, t: (b, t, 0)),
                  pl.BlockSpec((1, 1, modtab.shape[-1]), lambda b, t: (2 * b + jnp.minimum(t, 1), 0, 0)),
                  full(g), full(whg), full(wrw), full(wgt)],
        out_specs=[pl.BlockSpec((1, TILE, o.shape[-1]), lambda b, t: (b, t, 0)) for o in outs],
        out_shape=outs,
        compiler_params=_params("arbitrary", "arbitrary"),
        name="in_proj",
    )(xcat, modtab, g, whg, wrw, wgt)


def _scan_tile(d, s, nt, nctx):
    bwd = jnp.where(s < nctx, nctx - 1 - s, nt - 1 - (s - nctx))
    return jnp.where(d == 0, s, bwd)


def _hgrn_stack():
    c = CHUNK
    out = np.zeros((2, (LEVELS + 1) * c, c), np.float32)
    for d in range(2):
        pi = np.arange(c) if d == 0 else c - 1 - np.arange(c)
        out[d, :c] = pi[None, :] <= pi[:, None]
        for lv in range(1, LEVELS + 1):
            m = 1 << lv
            mid = (pi // m) * m + m // 2
            late = (pi % m) >= m // 2
            sum_late = (pi[None, :] >= mid[:, None]) & (pi[None, :] <= pi[:, None])
            sum_early = (pi[None, :] > pi[:, None]) & (pi[None, :] <= mid[:, None] - 1)
            out[d, lv * c:(lv + 1) * c] = np.where(late[:, None], sum_late, sum_early)
    return out


def _hgrn_kernel(q_ref, f_ref, i_ref, lb_ref, stack_ref, o_ref, st_ref, *, nt, nctx):
    d = pl.program_id(1)
    s = pl.program_id(2)
    C = CHUNK
    H = HG_HEAD

    @pl.when(s == 0)
    def _():
        st_ref[...] = jnp.zeros_like(st_ref)

    log_lb = lb_ref[0, 0:1, :]
    log1m_lb = lb_ref[0, 1:2, :]
    one_m_lb = lb_ref[0, 2:3, :]
    stack = stack_ref[0]

    row = lax.broadcasted_iota(jnp.int32, (C, H), 0)
    pi = jnp.where(d == 0, row, C - 1 - row)
    ii = lax.broadcasted_iota(jnp.int32, (C, C), 0)
    jj = lax.broadcasted_iota(jnp.int32, (C, C), 1)
    nchunk = TILE // C

    def chunk(j, carry):
        cj = jnp.where(d == 0, j, nchunk - 1 - j)
        r0 = pl.multiple_of(cj * C, C)
        xf = f_ref[0, pl.ds(r0, C), :]
        lsig = jnp.minimum(xf, 0.0) - jnp.log1p(jnp.exp(-jnp.abs(xf)))
        t2 = log1m_lb + lsig
        mx = jnp.maximum(log_lb, t2)
        logf = mx + jnp.log1p(jnp.exp(-jnp.abs(log_lb - t2)))
        kin = one_m_lb * _sigmoid(-xf)
        qr = q_ref[0, pl.ds(r0, C), :]
        q = qr * _sigmoid(qr) * (HG_HEAD ** -0.5)
        v = i_ref[0, pl.ds(r0, C), :]

        e_all = _dot_exact_lhs(stack, logf)
        cs = e_all[0:C]
        tot = jnp.sum(logf, axis=0, keepdims=True)
        q_in = q * jnp.exp(cs)
        k_end = kin * jnp.exp(tot - cs)
        dec_end = jnp.exp(tot)
        ex = [jnp.exp(e_all[lv * C:(lv + 1) * C]) for lv in range(1, LEVELS + 1)]

        for h in range(HG_HEADS):
            ls = slice(h * H, (h + 1) * H)
            qh, kh, vh = q[:, ls], kin[:, ls], v[:, ls]
            scores = jnp.where(ii == jj, _bdot_nt(qh, kh), 0.0)
            for lv in range(1, LEVELS + 1):
                late = ((pi >> (lv - 1)) & 1) == 1
                exh = ex[lv - 1][:, ls]
                qm = jnp.where(late, qh * exh, 0.0)
                km = jnp.where(late, 0.0, kh * exh)
                sc = _bdot_nt(qm, km)
                if lv < LEVELS:
                    sc = jnp.where((ii >> lv) == (jj >> lv), sc, 0.0)
                scores = scores + sc
            st = st_ref[h]
            o = _bdot_nt(q_in[:, ls], st) + _bdot(scores, vh)
            o_ref[0, 0, pl.ds(r0, C), ls] = o
            st_ref[h] = st * dec_end[:, ls] + _bdot_tn(vh, k_end[:, ls])
        return carry

    lax.fori_loop(0, nchunk, chunk, 0)


def _hgrn(p_hg, lbtab, stack, nctx):
    B, TC, _ = p_hg.shape
    nt = TC // TILE
    tmap = lambda col: (lambda b, d, s: (b, _scan_tile(d, s, nt, nctx), col))
    return pl.pallas_call(
        functools.partial(_hgrn_kernel, nt=nt, nctx=nctx),
        grid=(B, 2, nt),
        in_specs=[pl.BlockSpec((1, TILE, HG_WIDTH), tmap(0)),
                  pl.BlockSpec((1, TILE, HG_WIDTH), lambda b, d, s: (b, _scan_tile(d, s, nt, nctx), 1 + d)),
                  pl.BlockSpec((1, TILE, HG_WIDTH), tmap(3)),
                  pl.BlockSpec((1, 3, HG_WIDTH), lambda b, d, s: (d, 0, 0)),
                  pl.BlockSpec((1,) + stack.shape[1:], lambda b, d, s: (d, 0, 0))],
        out_specs=pl.BlockSpec((1, 1, TILE, HG_WIDTH),
                               lambda b, d, s: (d, b, _scan_tile(d, s, nt, nctx), 0)),
        out_shape=jax.ShapeDtypeStruct((2, B, TC, HG_WIDTH), F32),
        scratch_shapes=[pltpu.VMEM((HG_HEADS, HG_HEAD, HG_HEAD), F32)],
        compiler_params=_params("arbitrary", "arbitrary", "arbitrary"),
        name="hgrn2_scan",
    )(p_hg, p_hg, p_hg, lbtab, stack)


def _rw_tri():
    out = np.zeros((2, TILE, TILE), np.float32)
    idx = np.arange(TILE)
    same = (idx[:, None] // CHUNK) == (idx[None, :] // CHUNK)
    out[0] = same & (idx[None, :] <= idx[:, None])
    out[1] = same & (idx[None, :] >= idx[:, None])
    return out


def _head_ones():
    idx = np.arange(RW_WIDTH)
    return ((idx[:, None] // RW_HEAD) == (idx[None, :] // RW_HEAD)).astype(np.float32)


def _rwkv_kernel(cur_ref, up_ref, dn_ref, mu_ref, vec_ref, dvec_ref, w2_ref, a2_ref, g2_ref,
                 hones_ref, tri_ref, y_ref, bon_ref, g_ref,
                 s_ref, at_s, rt_s, bt_s, kt_s, v_s, tot_s, *, nt, nctx):
    d = pl.program_id(1)
    s = pl.program_id(2)
    t = _scan_tile(d, s, nt, nctx)
    C = CHUNK
    W = RW_WIDTH

    @pl.when(s == 0)
    def _():
        s_ref[...] = jnp.zeros_like(s_ref)

    p = cur_ref[0]
    is_ctx = t < nctx
    rowi = lax.broadcasted_iota(jnp.int32, p.shape, 0)
    lane = lax.broadcasted_iota(jnp.int32, p.shape, 1)
    prev1 = pltpu.roll(p, 1, 0)
    next1 = pltpu.roll(p, TILE - 1, 0)
    seg = jnp.where(is_ctx, TILE - 1, GRID_W - 1)
    prev1 = jnp.where((rowi & seg) == 0, 0.0, prev1)
    next1 = jnp.where((rowi & seg) == seg, 0.0, next1)
    up_ok = t > nctx
    dn_ok = jnp.logical_and(t >= nctx, t < nt - 1)
    up = jnp.concatenate([jnp.where(up_ok, up_ref[0], 0.0), p[:TILE - GRID_W]], axis=0)
    dn = jnp.concatenate([p[GRID_W:], jnp.where(dn_ok, dn_ref[0], 0.0)], axis=0)
    cls = lane & 3
    sh_even = jnp.where(cls == 0, prev1, jnp.where(is_ctx, prev1, up))
    sh_odd = jnp.where(cls == 1, next1, jnp.where(is_ctx, next1, dn))
    shifted = jnp.where((cls & 1) == 0, sh_even, sh_odd)
    pf = p + mu_ref[...] * (shifted - p)

    r = pf[:, 0:W]
    k = pf[:, W:2 * W]
    v = pf[:, 2 * W:3 * W]
    o3 = 3 * W
    wd = pf[:, o3:o3 + 2 * RW_LORA]
    ad = pf[:, o3 + 2 * RW_LORA:o3 + 4 * RW_LORA]
    gd = pf[:, o3 + 4 * RW_LORA:o3 + 4 * RW_LORA + RW_GATE_LORA]

    k_k = vec_ref[0:1, :]
    k_a = vec_ref[1:2, :]
    r_k = vec_ref[2:3, :]
    w0 = dvec_ref[0, 0:1, :]
    a0 = dvec_ref[0, 1:2, :]
    hones = hones_ref[...]

    lw = w0 + _bdot(jnp.tanh(wd), w2_ref[0])
    wlog = -_softplus(-lw) - 0.5
    ld = -jnp.exp(wlog)
    a = _sigmoid(a0 + _bdot(ad, a2_ref[0]))
    kkr = k * k_k
    ss = _dot_exact_rhs(kkr * kkr, hones)
    kk = kkr / jnp.maximum(jnp.sqrt(ss), 1e-12)
    kdir = k * (1.0 + (a - 1.0) * k_a)
    bdir = kk * a

    bon_ref[0, 0] = _dot_exact_rhs(r * kdir * r_k, hones) * v
    g_ref[0, 0] = _bdot(_sigmoid(gd), g2_ref[...])

    cs = _dot_exact_lhs(tri_ref[0], ld)
    einv = jnp.exp(-cs)
    at_s[...] = -kk * jnp.exp(cs - ld)
    rt_s[...] = r * jnp.exp(cs)
    bt_s[...] = bdir * einv
    kt_s[...] = kdir * einv
    v_s[...] = v
    nchunk = TILE // C
    for c in range(nchunk):
        tot_s[c] = jnp.broadcast_to(jnp.sum(ld[c * C:(c + 1) * C], axis=0, keepdims=True), (8, W))

    ii = lax.broadcasted_iota(jnp.int32, (C, C), 0)
    jj = lax.broadcasted_iota(jnp.int32, (C, C), 1)
    sgn = jnp.where(d == 0, 1, -1)
    before = (ii - jj) * sgn > 0
    upto = jnp.logical_or(before, ii == jj)
    eye = (ii == jj).astype(F32)
    pi_ = lax.broadcasted_iota(jnp.int32, (PAIR, PAIR), 0)
    pj_ = lax.broadcasted_iota(jnp.int32, (PAIR, PAIR), 1)
    same_head = (pi_ < RW_HEAD) == (pj_ < RW_HEAD)
    eye_pair = (pi_ == pj_).astype(F32)
    lane_c = lax.broadcasted_iota(jnp.int32, (C, PAIR), 1)
    head0 = lane_c < RW_HEAD

    def chunk(j, carry):
        cj = jnp.where(d == 0, j, nchunk - 1 - j)
        r0 = pl.multiple_of(cj * C, C)
        rows = pl.ds(r0, C)
        tot_c = tot_s[cj]
        for pr in range(RW_HEADS // 2):
            ls = slice(pr * PAIR, (pr + 1) * PAIR)
            at = at_s[rows, ls]
            rt = rt_s[rows, ls]
            bt = bt_s[rows, ls]
            kt = kt_s[rows, ls]
            vv = v_s[rows, ls]
            per_head = []
            for e in range(2):
                me = head0 if e == 0 else jnp.logical_not(head0)
                lhs = jnp.concatenate([jnp.where(me, at, 0.0), jnp.where(me, rt, 0.0)], axis=0)
                xb = _bdot_nt(lhs, bt)
                xk = _bdot_nt(lhs, kt)
                a_ab = jnp.where(before, xb[:C], 0.0)
                a_rb = jnp.where(upto, xb[C:], 0.0)
                a_ak = jnp.where(before, xk[:C], 0.0)
                a_rk = jnp.where(upto, xk[C:], 0.0)
                tinv = eye + a_ab
                pw = a_ab
                for _ in range(LEVELS - 1):
                    pw = _bdot(pw, pw)
                    tinv = tinv + _bdot(tinv, pw)
                ahat = _bdot(tinv, at)
                uv = _bdot(tinv, _bdot(a_ak, vv))
                per_head.append((a_rb, a_rk, ahat, uv))
            ahat = jnp.where(head0, per_head[0][2], per_head[1][2])
            uv = jnp.where(head0, per_head[0][3], per_head[1][3])
            rhat = rt + jnp.where(head0, _bdot(per_head[0][0], ahat), _bdot(per_head[1][0], ahat))
            yloc = jnp.where(head0,
                             _bdot(per_head[0][0], uv) + _bdot(per_head[0][1], vv),
                             _bdot(per_head[1][0], uv) + _bdot(per_head[1][1], vv))
            gam = jnp.exp(tot_c[0:1, ls])
            mm = jnp.where(same_head, eye_pair + _bdot_tn(ahat, bt), 0.0) * gam
            nn = jnp.where(same_head,
                           _bdot_tn(jnp.concatenate([uv, vv], axis=0),
                                    jnp.concatenate([bt, kt], axis=0)), 0.0) * gam
            st = s_ref[pr]
            y_ref[0, 0, rows, ls] = _bdot_nt(rhat, st) + yloc
            s_ref[pr] = _bdot(st, mm) + nn
        return carry

    lax.fori_loop(0, nchunk, chunk, 0)


def _rwkv(p_rw, mu, vecs, dvecs, w2cat, a2cat, g2, hones, tri, nctx):
    B, TC, _ = p_rw.shape
    nt = TC // TILE
    hb = TILE // GRID_W
    nhb = TC // GRID_W
    tile = lambda b, d, s: _scan_tile(d, s, nt, nctx)
    full = lambda a: pl.BlockSpec(a.shape, lambda b, d, s: (0,) * a.ndim)
    per_dir = lambda a: pl.BlockSpec((1,) + a.shape[1:], lambda b, d, s: (d,) + (0,) * (a.ndim - 1))
    out_spec = pl.BlockSpec((1, 1, TILE, RW_WIDTH), lambda b, d, s: (d, b, tile(b, d, s), 0))
    out_shape = jax.ShapeDtypeStruct((2, B, TC, RW_WIDTH), F32)
    tw = pltpu.VMEM((TILE, RW_WIDTH), F32)
    return pl.pallas_call(
        functools.partial(_rwkv_kernel, nt=nt, nctx=nctx),
        grid=(B, 2, nt),
        in_specs=[pl.BlockSpec((1, TILE, RW_COLS), lambda b, d, s: (b, tile(b, d, s), 0)),
                  pl.BlockSpec((1, GRID_W, RW_COLS),
                               lambda b, d, s: (b, jnp.maximum(tile(b, d, s) * hb - 1, 0), 0)),
                  pl.BlockSpec((1, GRID_W, RW_COLS),
                               lambda b, d, s: (b, jnp.minimum((tile(b, d, s) + 1) * hb, nhb - 1), 0)),
                  full(mu), full(vecs), per_dir(dvecs), per_dir(w2cat), per_dir(a2cat), full(g2),
                  full(hones), per_dir(tri)],
        out_specs=[out_spec, out_spec, out_spec],
        out_shape=[out_shape, out_shape, out_shape],
        scratch_shapes=[pltpu.VMEM((RW_HEADS // 2, PAIR, PAIR), F32), tw, tw, tw, tw, tw,
                        pltpu.VMEM((TILE // CHUNK, 8, RW_WIDTH), F32)],
        compiler_params=_params("arbitrary", "arbitrary", "arbitrary"),
        name="rwkv7_scan",
    )(p_rw, p_rw, p_rw, mu, vecs, dvecs, w2cat, a2cat, g2, hones, tri)


def _merge_kernel(ohg_ref, og_ref, y_ref, bon_ref, g_ref, gt_ref, x_ref, mod_ref,
                  hgn_ref, gn_ref, ng_ref, hones_ref, pa_ref, pb_ref, wo_ref, o_ref):
    D = x_ref.shape[-1]
    H = HG_HEAD
    oh = ohg_ref[0, 0] + ohg_ref[1, 0]
    og = og_ref[0]
    hgn = hgn_ref[...]
    parts = []
    for h in range(HG_HEADS):
        ls = slice(h * H, (h + 1) * H)
        parts.append(_rms(oh[:, ls], hgn) * (og[:, ls] * _sigmoid(og[:, ls])))
    o_hg = jnp.concatenate(parts, axis=-1)
    hones = hones_ref[...]
    y = y_ref[0, 0] + y_ref[1, 0]
    mean = _dot_exact_rhs(y, hones) * (1.0 / RW_HEAD)
    yc = y - mean
    var = _dot_exact_rhs(yc * yc, hones) * (1.0 / RW_HEAD)
    yn = yc * lax.rsqrt(var + RW_GN_EPS) * gn_ref[0:1, :] + gn_ref[1:2, :]
    o_rw = (yn + bon_ref[0, 0] + bon_ref[1, 0]) * g_ref[0, 0]
    gt = gt_ref[0]
    m = (_sigmoid(gt[:, :D]) * _bdot(o_hg, pa_ref[...])
         + _sigmoid(gt[:, D:]) * _bdot(o_rw, pb_ref[...]))
    yx = _bdot(m, wo_ref[...])
    gate = mod_ref[0, :, 2 * D:3 * D]
    o_ref[0] = x_ref[0] + gate * _rms(yx, ng_ref[...])


def _merge(ohg, p_hg, y, bon, g, p_gt, xcat, modtab, hgn, gn, ng, hones, pa, pb, wo, t0):
    B, TC, D = xcat.shape
    nt = TC // TILE - t0
    full = lambda a: pl.BlockSpec(a.shape, lambda b, t: (0,) * a.ndim)
    two = pl.BlockSpec((2, 1, TILE, HG_WIDTH), lambda b, t: (0, b, t + t0, 0))
    return pl.pallas_call(
        _merge_kernel,
        grid=(B, nt),
        in_specs=[two,
                  pl.BlockSpec((1, TILE, HG_WIDTH), lambda b, t: (b, t + t0, 4)),
                  two, two,
                  pl.BlockSpec((1, 1, TILE, RW_WIDTH), lambda b, t: (0, b, t + t0, 0)),
                  pl.BlockSpec((1, TILE, 2 * D), lambda b, t: (b, t + t0, 0)),
                  pl.BlockSpec((1, TILE, D), lambda b, t: (b, t + t0, 0)),
                  pl.BlockSpec((1, 1, modtab.shape[-1]),
                               lambda b, t: (2 * b + jnp.minimum(t + t0, 1), 0, 0)),
                  full(hgn), full(gn), full(ng), full(hones), full(pa), full(pb), full(wo)],
        out_specs=pl.BlockSpec((1, TILE, D), lambda b, t: (b, t, 0)),
        out_shape=jax.ShapeDtypeStruct((B, nt * TILE, D), F32),
        compiler_params=_params("arbitrary", "arbitrary"),
        name="gated_merge",
    )(ohg, p_hg, y, bon, g, p_gt, xcat, modtab, hgn, gn, ng, hones, pa, pb, wo)


def _ffn_kernel(x_ref, mod_ref, g_ref, w1_ref, w2_ref, o_ref):
    D = x_ref.shape[-1]
    F = w2_ref.shape[0]
    x = x_ref[0]
    h = _rms(x, g_ref[0:1, :]) * (1.0 + mod_ref[0, :, 4 * D:5 * D]) + mod_ref[0, :, 3 * D:4 * D]
    hb = h.astype(BF16)
    gt = jnp.dot(hb, w1_ref[:, 0:F], preferred_element_type=F32)
    up = jnp.dot(hb, w1_ref[:, F:2 * F], preferred_element_type=F32)
    act = (gt * _sigmoid(gt) * up).astype(BF16)
    y = jnp.dot(act, w2_ref[...], preferred_element_type=F32)
    o_ref[0] = x + mod_ref[0, :, 5 * D:6 * D] * _rms(y, g_ref[1:2, :])


def _ffn(x, modtab, g, w1, w2, t0):
    B, TX, D = x.shape
    nt = TX // TILE
    full = lambda a: pl.BlockSpec(a.shape, lambda b, t: (0,) * a.ndim)
    return pl.pallas_call(
        _ffn_kernel,
        grid=(B, nt),
        in_specs=[pl.BlockSpec((1, TILE, D), lambda b, t: (b, t, 0)),
                  pl.BlockSpec((1, 1, modtab.shape[-1]),
                               lambda b, t: (2 * b + jnp.minimum(t + t0, 1), 0, 0)),
                  full(g), full(w1), full(w2)],
        out_specs=pl.BlockSpec((1, TILE, D), lambda b, t: (b, t, 0)),
        out_shape=jax.ShapeDtypeStruct((B, TX, D), F32),
        compiler_params=_params("arbitrary", "arbitrary"),
        name="swiglu_ffn",
    )(x, modtab, g, w1, w2)


def kernel(x, c, ctx, c_ctx, ada_w, ada_b, norm_g, w_in, hg_lb_logits, hg_norm_g, rw_mu, rw_w0, rw_w2,
           rw_a0, rw_a2, rw_g2, rw_kk, rw_ka, rw_rk, rw_gn_g, rw_gn_b, proj_a, proj_b, w_out, ffn_w1,
           ffn_w2):
    B, T, D = x.shape
    LCTX = ctx.shape[1]
    L = ada_w.shape[0]
    assert T % TILE == 0 and LCTX == TILE and TILE % GRID_W == 0 and T // TILE >= 2
    nctx = LCTX // TILE

    rows = -(-(B + 1) // 8) * 8
    cond = jnp.zeros((rows, D), F32).at[:B].set(c).at[B].set(c_ctx)
    mods = _mods(cond, ada_w, ada_b)

    lb_cum = jnp.cumsum(jax.nn.softmax(hg_lb_logits.astype(F32), axis=0), axis=0)
    hg_lb = lb_cum - lb_cum[:1]
    lbtab = jnp.stack([jnp.log(hg_lb), jnp.log1p(-hg_lb), 1.0 - hg_lb], axis=2)

    stack = jnp.asarray(_hgrn_stack(), BF16)
    tri = jnp.asarray(_rw_tri(), BF16)
    hones = jnp.asarray(_head_ones(), BF16)

    zeros = jnp.zeros_like(rw_w2[:, 0])
    w2cat = jnp.stack([jnp.concatenate([rw_w2[:, 0], zeros], axis=1),
                       jnp.concatenate([zeros, rw_w2[:, 1]], axis=1)], axis=1).astype(BF16)
    a2cat = jnp.stack([jnp.concatenate([rw_a2[:, 0], zeros], axis=1),
                       jnp.concatenate([zeros, rw_a2[:, 1]], axis=1)], axis=1).astype(BF16)

    xcat = jnp.concatenate([ctx, x], axis=1)
    for l in range(L):
        last = l == L - 1
        lat = mods[l, :B]
        ctxm = jnp.broadcast_to(mods[l, B], lat.shape)
        modtab = jnp.stack([ctxm, lat], axis=1).reshape(2 * B, 1, 6 * D)
        wl = w_in[l].astype(BF16)
        p_hg, p_rw, p_gt = _inproj(xcat, modtab, norm_g[l, 0:1], wl[:, :HG_COLS],
                                   wl[:, HG_COLS:HG_COLS + RW_COLS], wl[:, HG_COLS + RW_COLS:])
        ohg = _hgrn(p_hg, lbtab[l], stack, nctx)
        vecs = jnp.stack([rw_kk[l], rw_ka[l], rw_rk[l].reshape(-1)], axis=0)
        dvecs = jnp.stack([rw_w0[l], rw_a0[l]], axis=1)
        y, bon, g = _rwkv(p_rw, rw_mu[l][None, :], vecs, dvecs, w2cat[l], a2cat[l],
                          rw_g2[l].astype(BF16), hones, tri, nctx)
        t0 = nctx if last else 0
        xm = _merge(ohg, p_hg, y, bon, g, p_gt, xcat, modtab, hg_norm_g[l][None, :],
                    jnp.stack([rw_gn_g[l], rw_gn_b[l]], axis=0), norm_g[l, 1:2], hones,
                    proj_a[l].astype(BF16), proj_b[l].astype(BF16), w_out[l].astype(BF16), t0)
        xcat = _ffn(xm, modtab, norm_g[l, 2:4], ffn_w1[l].astype(BF16), ffn_w2[l].astype(BF16), t0)
    return xcat
```

```python
import functools

import numpy as np
import jax
import jax.numpy as jnp
from jax import lax
from jax.experimental import pallas as pl
from jax.experimental.pallas import tpu as pltpu

F32 = jnp.float32
BF16 = jnp.bfloat16

NORM_EPS = 1e-6
GRID_W = 64
HG_HEADS = 4
HG_HEAD = 128
HG_WIDTH = HG_HEADS * HG_HEAD
RW_HEADS = 8
RW_HEAD = 64
RW_WIDTH = RW_HEADS * RW_HEAD
RW_LORA = 64
RW_GATE_LORA = 128
RW_GN_EPS = 64e-5
RW_COLS = 3 * RW_WIDTH + 4 * RW_LORA + RW_GATE_LORA
HG_COLS = 5 * HG_WIDTH

TILE = 256
CHUNK = 64
LEVELS = 6
HG_MM_LEVELS = 3
HG_UNROLL = 4
PAIR = 2 * RW_HEAD
RW_UNROLL = 4
VMEM_LIMIT = 56 * 1024 * 1024


def _bdot(a, b):
    return jnp.dot(a.astype(BF16), b.astype(BF16), preferred_element_type=F32)


def _bdot_nt(a, b):
    return lax.dot_general(a.astype(BF16), b.astype(BF16), (((1,), (1,)), ((), ())),
                           preferred_element_type=F32)


def _bdot_tn(a, b):
    return lax.dot_general(a.astype(BF16), b.astype(BF16), (((0,), (0,)), ((), ())),
                           preferred_element_type=F32)


def _split3(x):
    hi = x.astype(BF16)
    r1 = x - hi.astype(F32)
    mid = r1.astype(BF16)
    lo = (r1 - mid.astype(F32)).astype(BF16)
    return hi, mid, lo


def _dot_exact_lhs(m01, x, terms=2):
    d = lambda t: jnp.dot(m01, t, preferred_element_type=F32)
    return sum(d(t) for t in _split3(x)[:terms])


def _dot_exact_rhs(x, m01, terms=2):
    d = lambda t: jnp.dot(t, m01, preferred_element_type=F32)
    return sum(d(t) for t in _split3(x)[:terms])


def _rms(x, g):
    ms = jnp.mean(x * x, axis=-1, keepdims=True)
    return x * lax.rsqrt(ms + NORM_EPS) * g


def _sigmoid(x):
    return 0.5 * jnp.tanh(0.5 * x) + 0.5


def _params(*sem):
    return pltpu.CompilerParams(dimension_semantics=sem, vmem_limit_bytes=VMEM_LIMIT)


def _mods_kernel(c_ref, w_ref, b_ref, o_ref):
    c = c_ref[...]
    s = c * _sigmoid(c)
    sh, sm, sl = _split3(s)
    wh, wm, wl = _split3(w_ref[0])
    d = lambda a, b: jnp.dot(a, b, preferred_element_type=F32)
    acc = d(sh, wh) + d(sh, wm) + d(sm, wh) + d(sh, wl) + d(sl, wh) + d(sm, wm)
    o_ref[0] = acc + b_ref[0]


def _mods(cond, ada_w, ada_b):
    L, D, N = ada_w.shape
    rows = cond.shape[0]
    tn = 1536
    return pl.pallas_call(
        _mods_kernel,
        grid=(L, N // tn),
        in_specs=[pl.BlockSpec((rows, D), lambda l, n: (0, 0)),
                  pl.BlockSpec((1, D, tn), lambda l, n: (l, 0, n)),
                  pl.BlockSpec((1, 1, tn), lambda l, n: (l, 0, n))],
        out_specs=pl.BlockSpec((1, rows, tn), lambda l, n: (l, 0, n)),
        out_shape=jax.ShapeDtypeStruct((L, rows, N), F32),
        compiler_params=_params("arbitrary", "arbitrary"),
        name="adaln_mods",
    )(cond, ada_w, ada_b.reshape(L, 1, N))


def _inproj_kernel(x_ref, mod_ref, g_ref, whg_ref, wrw_ref, wgt_ref, ohg_ref, orw_ref, ogt_ref):
    D = x_ref.shape[-1]
    y = _rms(x_ref[0], g_ref[...])
    shift = mod_ref[0, :, 0:D]
    scale = mod_ref[0, :, D:2 * D]
    h = (y * (1.0 + scale) + shift).astype(BF16)
    ohg_ref[0] = jnp.dot(h, whg_ref[...], preferred_element_type=F32)
    orw_ref[0] = jnp.dot(h, wrw_ref[...], preferred_element_type=F32)
    ogt_ref[0] = jnp.dot(h, wgt_ref[...], preferred_element_type=F32)


def _inproj(xcat, modtab, g, whg, wrw, wgt):
    B, TC, D = xcat.shape
    nt = TC // TILE
    full = lambda a: pl.BlockSpec(a.shape, lambda b, t: (0,) * a.ndim)
    outs = [jax.ShapeDtypeStruct((B, TC, w.shape[1]), F32) for w in (whg, wrw, wgt)]
    return pl.pallas_call(
        _inproj_kernel,
        grid=(B, nt),
        in_specs=[pl.BlockSpec((1, TILE, D), lambda b, t: (b, t, 0)),
                  pl.BlockSpec((1, 1, modtab.shape[-1]), lambda b, t: (2 * b + jnp.minimum(t, 1), 0, 0)),
                  full(g), full(whg), full(wrw), full(wgt)],
        out_specs=[pl.BlockSpec((1, TILE, o.shape[-1]), lambda b, t: (b, t, 0)) for o in outs],
        out_shape=outs,
        compiler_params=_params("arbitrary", "arbitrary"),
        name="in_proj",
    )(xcat, modtab, g, whg, wrw, wgt)


def _scan_tile(d, s, nt, nctx):
    bwd = jnp.where(s < nctx, nctx - 1 - s, nt - 1 - (s - nctx))
    return jnp.where(d == 0, s, bwd)


def _hgrn_stack():
    c = CHUNK
    out = np.zeros((2, (HG_MM_LEVELS + 1) * c, c), np.float32)
    for d in range(2):
        pi = np.arange(c) if d == 0 else c - 1 - np.arange(c)
        out[d, :c] = pi[None, :] <= pi[:, None]
        for lv in range(1, HG_MM_LEVELS + 1):
            m = 1 << lv
            mid = (pi // m) * m + m // 2
            late = (pi % m) >= m // 2
            sum_late = (pi[None, :] >= mid[:, None]) & (pi[None, :] <= pi[:, None])
            sum_early = (pi[None, :] > pi[:, None]) & (pi[None, :] <= mid[:, None] - 1)
            out[d, lv * c:(lv + 1) * c] = np.where(late[:, None], sum_late, sum_early)
    return out


def _hgrn_kernel(q_ref, f_ref, i_ref, lb_ref, stack_ref, o_ref, st_ref, *, nt, nctx):
    d = pl.program_id(1)
    s = pl.program_id(2)
    C = CHUNK
    H = HG_HEAD

    @pl.when(s == 0)
    def _():
        st_ref[...] = jnp.zeros_like(st_ref)

    log_lb = lb_ref[0, 0:1, :]
    log1m_lb = lb_ref[0, 1:2, :]
    one_m_lb = lb_ref[0, 2:3, :]
    stack = stack_ref[0]

    row = lax.broadcasted_iota(jnp.int32, (C, HG_WIDTH), 0)
    pi = jnp.where(d == 0, row, C - 1 - row)
    ii = lax.broadcasted_iota(jnp.int32, (C, C), 0)
    jj = lax.broadcasted_iota(jnp.int32, (C, C), 1)
    nchunk = TILE // C
    hsl = [slice(h * H, (h + 1) * H) for h in range(HG_HEADS)]
    ntdot = lambda a, b: lax.dot_general(a, b, (((1,), (1,)), ((), ())), preferred_element_type=F32)

    def chunk_local(cj):
        r0 = pl.multiple_of(cj * C, C)
        xf = f_ref[0, pl.ds(r0, C), :]
        lsig = jnp.minimum(xf, 0.0) - jnp.log(1.0 + jnp.exp(-jnp.abs(xf)))
        t2 = log1m_lb + lsig
        mx = jnp.maximum(log_lb, t2)
        logf = mx + jnp.log(1.0 + jnp.exp(-jnp.abs(log_lb - t2)))
        kin = one_m_lb * _sigmoid(-xf)
        qr = q_ref[0, pl.ds(r0, C), :]
        q = qr * _sigmoid(qr) * (HG_HEAD ** -0.5)
        vb = i_ref[0, pl.ds(r0, C), :].astype(BF16)

        e_all = _dot_exact_lhs(stack, logf)
        cs = e_all[0:C]
        tot = jnp.sum(logf, axis=0, keepdims=True)
        qb = q.astype(BF16)
        kb = kin.astype(BF16)
        scores = [jnp.where(ii == jj, ntdot(qb[:, ls], kb[:, ls]), 0.0) for ls in hsl]
        for lv in range(1, LEVELS + 1):
            m = 1 << lv
            late = ((pi >> (lv - 1)) & 1) == 1
            if lv <= HG_MM_LEVELS:
                e = e_all[lv * C:(lv + 1) * C]
            else:
                cmid = jnp.concatenate(
                    [jnp.broadcast_to(jnp.where(d == 0, cs[b * m + m // 2 - 1:b * m + m // 2],
                                                cs[b * m + m // 2:b * m + m // 2 + 1]),
                                      (m, HG_WIDTH)) for b in range(C // m)], axis=0)
                e = jnp.where(late, cs - cmid, cmid - cs)
            ex = jnp.exp(e)
            qm = jnp.where(late, q * ex, 0.0).astype(BF16)
            km = jnp.where(late, 0.0, kin * ex).astype(BF16)
            sc = [ntdot(qm[:, ls], km[:, ls]) for ls in hsl]
            if lv < LEVELS:
                same = (ii >> lv) == (jj >> lv)
                sc = [jnp.where(same, x, 0.0) for x in sc]
            scores = [a + b for a, b in zip(scores, sc)]
        q_in = (q * jnp.exp(cs)).astype(BF16)
        k_end = (kin * jnp.exp(tot - cs)).astype(BF16)
        o_loc = [jnp.dot(scores[h].astype(BF16), vb[:, ls], preferred_element_type=F32)
                 for h, ls in enumerate(hsl)]
        kv = [lax.dot_general(vb[:, ls], k_end[:, ls], (((0,), (0,)), ((), ())),
                              preferred_element_type=F32) for ls in hsl]
        return r0, q_in, o_loc, kv, jnp.exp(tot)

    def chunks(j, carry):
        loc = []
        for u in range(HG_UNROLL):
            step = j * HG_UNROLL + u
            loc.append(chunk_local(jnp.where(d == 0, step, nchunk - 1 - step)))
        st = [st_ref[h] for h in range(HG_HEADS)]
        for r0, q_in, o_loc, kv, dec_end in loc:
            for h, ls in enumerate(hsl):
                o_ref[0, 0, pl.ds(r0, C), ls] = ntdot(q_in[:, ls], st[h].astype(BF16)) + o_loc[h]
            st = [st[h] * dec_end[:, ls] + kv[h] for h, ls in enumerate(hsl)]
        for h in range(HG_HEADS):
            st_ref[h] = st[h]
        return carry

    lax.fori_loop(0, nchunk // HG_UNROLL, chunks, 0)


def _hgrn(p_hg, lbtab, stack, nctx):
    B, TC, _ = p_hg.shape
    nt = TC // TILE
    tmap = lambda col: (lambda b, d, s: (b, _scan_tile(d, s, nt, nctx), col))
    return pl.pallas_call(
        functools.partial(_hgrn_kernel, nt=nt, nctx=nctx),
        grid=(B, 2, nt),
        in_specs=[pl.BlockSpec((1, TILE, HG_WIDTH), tmap(0)),
                  pl.BlockSpec((1, TILE, HG_WIDTH), lambda b, d, s: (b, _scan_tile(d, s, nt, nctx), 1 + d)),
                  pl.BlockSpec((1, TILE, HG_WIDTH), tmap(3)),
                  pl.BlockSpec((1, 3, HG_WIDTH), lambda b, d, s: (d, 0, 0)),
                  pl.BlockSpec((1,) + stack.shape[1:], lambda b, d, s: (d, 0, 0))],
        out_specs=pl.BlockSpec((1, 1, TILE, HG_WIDTH),
                               lambda b, d, s: (d, b, _scan_tile(d, s, nt, nctx), 0)),
        out_shape=jax.ShapeDtypeStruct((2, B, TC, HG_WIDTH), F32),
        scratch_shapes=[pltpu.VMEM((HG_HEADS, HG_HEAD, HG_HEAD), F32)],
        compiler_params=_params("arbitrary", "arbitrary", "arbitrary"),
        name="hgrn2_scan",
    )(p_hg, p_hg, p_hg, lbtab, stack)


def _rw_tri():
    out = np.zeros((2, TILE, TILE), np.float32)
    idx = np.arange(TILE)
    same = (idx[:, None] // CHUNK) == (idx[None, :] // CHUNK)
    out[0] = same & (idx[None, :] <= idx[:, None])
    out[1] = same & (idx[None, :] >= idx[:, None])
    return out


def _head_ones():
    idx = np.arange(RW_WIDTH)
    return ((idx[:, None] // RW_HEAD) == (idx[None, :] // RW_HEAD)).astype(np.float32)


def _rwkv_kernel(cur_ref, up_ref, dn_ref, mu_ref, vec_ref, dvec_ref, w2_ref, a2_ref, g2_ref,
                 hones_ref, tri_ref, y_ref, bon_ref, g_ref,
                 s_ref, at_s, rt_s, bt_s, kt_s, v_s, tot_s, *, nt, nctx):
    d = pl.program_id(1)
    s = pl.program_id(2)
    t = _scan_tile(d, s, nt, nctx)
    C = CHUNK
    W = RW_WIDTH

    @pl.when(s == 0)
    def _():
        s_ref[...] = jnp.zeros_like(s_ref)

    p = cur_ref[0]
    is_ctx = t < nctx
    rowi = lax.broadcasted_iota(jnp.int32, p.shape, 0)
    lane = lax.broadcasted_iota(jnp.int32, p.shape, 1)
    prev1 = pltpu.roll(p, 1, 0)
    next1 = pltpu.roll(p, TILE - 1, 0)
    seg = jnp.where(is_ctx, TILE - 1, GRID_W - 1)
    prev1 = jnp.where((rowi & seg) == 0, 0.0, prev1)
    next1 = jnp.where((rowi & seg) == seg, 0.0, next1)
    up_ok = t > nctx
    dn_ok = jnp.logical_and(t >= nctx, t < nt - 1)
    up = jnp.concatenate([jnp.where(up_ok, up_ref[0], 0.0), p[:TILE - GRID_W]], axis=0)
    dn = jnp.concatenate([p[GRID_W:], jnp.where(dn_ok, dn_ref[0], 0.0)], axis=0)
    cls = lane & 3
    sh_even = jnp.where(cls == 0, prev1, jnp.where(is_ctx, prev1, up))
    sh_odd = jnp.where(cls == 1, next1, jnp.where(is_ctx, next1, dn))
    shifted = jnp.where((cls & 1) == 0, sh_even, sh_odd)
    pf = p + mu_ref[...] * (shifted - p)

    r = pf[:, 0:W]
    k = pf[:, W:2 * W]
    v = pf[:, 2 * W:3 * W]
    o3 = 3 * W
    wd = pf[:, o3:o3 + 2 * RW_LORA]
    ad = pf[:, o3 + 2 * RW_LORA:o3 + 4 * RW_LORA]
    gd = pf[:, o3 + 4 * RW_LORA:o3 + 4 * RW_LORA + RW_GATE_LORA]

    k_k = vec_ref[0:1, :]
    k_a = vec_ref[1:2, :]
    r_k = vec_ref[2:3, :]
    w0 = dvec_ref[0, 0:1, :]
    a0 = dvec_ref[0, 1:2, :]
    hones = hones_ref[...]

    lw = w0 + _bdot(jnp.tanh(wd), w2_ref[0])
    ld = -float(np.exp(-0.5)) * _sigmoid(lw)
    a = _sigmoid(a0 + _bdot(ad, a2_ref[0]))
    kkr = k * k_k
    ss = _dot_exact_rhs(kkr * kkr, hones)
    kk = kkr * lax.rsqrt(jnp.maximum(ss, 1e-24))
    kdir = k * (1.0 + (a - 1.0) * k_a)
    bdir = kk * a

    bon_ref[0, 0] = _dot_exact_rhs(r * kdir * r_k, hones) * v
    g_ref[0, 0] = _bdot(_sigmoid(gd), g2_ref[...])

    cs = _dot_exact_lhs(tri_ref[0], ld)
    einv = jnp.exp(-cs)
    at_s[...] = -kk * jnp.exp(cs - ld)
    rt_s[...] = r * jnp.exp(cs)
    bt_s[...] = bdir * einv
    kt_s[...] = kdir * einv
    v_s[...] = v
    nchunk = TILE // C
    for c in range(nchunk):
        tot_s[c] = jnp.broadcast_to(jnp.sum(ld[c * C:(c + 1) * C], axis=0, keepdims=True), (8, W))

    ii = lax.broadcasted_iota(jnp.int32, (C, C), 0)
    jj = lax.broadcasted_iota(jnp.int32, (C, C), 1)
    sgn = jnp.where(d == 0, 1, -1)
    eye = (ii == jj).astype(F32)
    pi_ = lax.broadcasted_iota(jnp.int32, (PAIR, PAIR), 0)
    pj_ = lax.broadcasted_iota(jnp.int32, (PAIR, PAIR), 1)
    same_head = (pi_ < RW_HEAD) == (pj_ < RW_HEAD)
    eye_pair = (pi_ == pj_).astype(F32)
    lane_c = lax.broadcasted_iota(jnp.int32, (C, PAIR), 1)
    head0 = lane_c < RW_HEAD

    ii2 = lax.broadcasted_iota(jnp.int32, (C, 2 * C), 0)
    jj2 = lax.broadcasted_iota(jnp.int32, (C, 2 * C), 1) & (C - 1)
    before2 = (ii2 - jj2) * sgn > 0
    upto2 = (ii2 - jj2) * sgn >= 0
    zpad = jnp.zeros((C, PAIR), BF16)
    npair = RW_HEADS // 2
    lsl = [slice(pr * PAIR, (pr + 1) * PAIR) for pr in range(npair)]
    pairs = [(u, pr) for u in range(RW_UNROLL) for pr in range(npair)]
    systems = [(u, pr, e) for u, pr in pairs for e in range(2)]
    fdot = lambda a, b: jnp.dot(a, b, preferred_element_type=F32)

    def chunk(j, carry):
        rows, tot_c = [], []
        for u in range(RW_UNROLL):
            step = j * RW_UNROLL + u
            cj = jnp.where(d == 0, step, nchunk - 1 - step)
            rows.append(pl.ds(pl.multiple_of(cj * C, C), C))
            tot_c.append(tot_s[cj])
        at = {(u, pr): at_s[rows[u], lsl[pr]] for u, pr in pairs}
        rt = {(u, pr): rt_s[rows[u], lsl[pr]] for u, pr in pairs}
        bt = {(u, pr): bt_s[rows[u], lsl[pr]].astype(BF16) for u, pr in pairs}
        kt = {(u, pr): kt_s[rows[u], lsl[pr]].astype(BF16) for u, pr in pairs}
        vv = {(u, pr): v_s[rows[u], lsl[pr]].astype(BF16) for u, pr in pairs}
        atb = {pq: at[pq].astype(BF16) for pq in pairs}
        bk = {pq: jnp.concatenate([bt[pq], kt[pq]], axis=0) for pq in pairs}
        top, bot = {}, {}
        for u, pr, e in systems:
            me = head0 if e == 0 else jnp.logical_not(head0)
            lhs = jnp.concatenate([jnp.where(me, at[u, pr], 0.0), jnp.where(me, rt[u, pr], 0.0)],
                                  axis=0)
            x = _bdot_nt(lhs, bk[u, pr])
            top[u, pr, e] = jnp.where(before2, x[:C], 0.0)
            bot[u, pr, e] = jnp.where(upto2, x[C:], 0.0).astype(BF16)
        pw = {sy: top[sy][:, :C] for sy in systems}
        tinv = {sy: eye + pw[sy] for sy in systems}
        for _ in range(LEVELS - 1):
            pwb = {sy: pw[sy].astype(BF16) for sy in systems}
            pw = {sy: fdot(pwb[sy], pwb[sy]) for sy in systems}
            tinv = {sy: tinv[sy] + _bdot(tinv[sy], pw[sy]) for sy in systems}
        tb = {sy: tinv[sy].astype(BF16) for sy in systems}
        w = {(u, pr, e): _bdot(top[u, pr, e], jnp.concatenate([zpad, vv[u, pr]], axis=0))
             for u, pr, e in systems}
        ah = {(u, pr, e): fdot(tb[u, pr, e], atb[u, pr]) for u, pr, e in systems}
        uh = {sy: _bdot(tb[sy], w[sy]) for sy in systems}
        ahat = {(u, pr): jnp.where(head0, ah[u, pr, 0], ah[u, pr, 1]).astype(BF16) for u, pr in pairs}
        uv = {(u, pr): jnp.where(head0, uh[u, pr, 0], uh[u, pr, 1]).astype(BF16) for u, pr in pairs}
        uvv = {pq: jnp.concatenate([uv[pq], vv[pq]], axis=0) for pq in pairs}
        rh = {(u, pr, e): fdot(bot[u, pr, e][:, :C], ahat[u, pr]) for u, pr, e in systems}
        yl = {(u, pr, e): fdot(bot[u, pr, e], uvv[u, pr]) for u, pr, e in systems}
        rhat = {(u, pr): rt[u, pr] + jnp.where(head0, rh[u, pr, 0], rh[u, pr, 1]) for u, pr in pairs}
        gam = {(u, pr): jnp.exp(tot_c[u][0:1, lsl[pr]]) for u, pr in pairs}
        mm = {pq: jnp.where(same_head, eye_pair + _bdot_tn(ahat[pq], bt[pq]), 0.0) * gam[pq]
              for pq in pairs}
        nn = {pq: jnp.where(same_head, _bdot_tn(uvv[pq], bk[pq]), 0.0) * gam[pq] for pq in pairs}
        st = [s_ref[pr] for pr in range(npair)]
        for u in range(RW_UNROLL):
            stb = [x.astype(BF16) for x in st]
            for pr in range(npair):
                y_ref[0, 0, rows[u], lsl[pr]] = (_bdot_nt(rhat[u, pr], stb[pr])
                                                 + jnp.where(head0, yl[u, pr, 0], yl[u, pr, 1]))
            st = [_bdot(stb[pr], mm[u, pr]) + nn[u, pr] for pr in range(npair)]
        for pr in range(npair):
            s_ref[pr] = st[pr]
        return carry

    lax.fori_loop(0, nchunk // RW_UNROLL, chunk, 0)


def _rwkv(p_rw, mu, vecs, dvecs, w2cat, a2cat, g2, hones, tri, nctx):
    B, TC, _ = p_rw.shape
    nt = TC // TILE
    hb = TILE // GRID_W
    nhb = TC // GRID_W
    tile = lambda b, d, s: _scan_tile(d, s, nt, nctx)
    full = lambda a: pl.BlockSpec(a.shape, lambda b, d, s: (0,) * a.ndim)
    per_dir = lambda a: pl.BlockSpec((1,) + a.shape[1:], lambda b, d, s: (d,) + (0,) * (a.ndim - 1))
    out_spec = pl.BlockSpec((1, 1, TILE, RW_WIDTH), lambda b, d, s: (d, b, tile(b, d, s), 0))
    out_shape = jax.ShapeDtypeStruct((2, B, TC, RW_WIDTH), F32)
    tw = pltpu.VMEM((TILE, RW_WIDTH), F32)
    return pl.pallas_call(
        functools.partial(_rwkv_kernel, nt=nt, nctx=nctx),
        grid=(B, 2, nt),
        in_specs=[pl.BlockSpec((1, TILE, RW_COLS), lambda b, d, s: (b, tile(b, d, s), 0)),
                  pl.BlockSpec((1, GRID_W, RW_COLS),
                               lambda b, d, s: (b, jnp.maximum(tile(b, d, s) * hb - 1, 0), 0)),
                  pl.BlockSpec((1, GRID_W, RW_COLS),
                               lambda b, d, s: (b, jnp.minimum((tile(b, d, s) + 1) * hb, nhb - 1), 0)),
                  full(mu), full(vecs), per_dir(dvecs), per_dir(w2cat), per_dir(a2cat), full(g2),
                  full(hones), per_dir(tri)],
        out_specs=[out_spec, out_spec, out_spec],
        out_shape=[out_shape, out_shape, out_shape],
        scratch_shapes=[pltpu.VMEM((RW_HEADS // 2, PAIR, PAIR), F32), tw, tw, tw, tw, tw,
                        pltpu.VMEM((TILE // CHUNK, 8, RW_WIDTH), F32)],
        compiler_params=_params("arbitrary", "arbitrary", "arbitrary"),
        name="rwkv7_scan",
    )(p_rw, p_rw, p_rw, mu, vecs, dvecs, w2cat, a2cat, g2, hones, tri)


def _merge_kernel(ohg_ref, og_ref, y_ref, bon_ref, g_ref, gt_ref, x_ref, mod_ref,
                  hgn_ref, gn_ref, ng_ref, hones_ref, pa_ref, pb_ref, wo_ref, o_ref):
    D = x_ref.shape[-1]
    H = HG_HEAD
    oh = ohg_ref[0, 0] + ohg_ref[1, 0]
    og = og_ref[0]
    hgn = hgn_ref[...]
    parts = []
    for h in range(HG_HEADS):
        ls = slice(h * H, (h + 1) * H)
        parts.append(_rms(oh[:, ls], hgn) * (og[:, ls] * _sigmoid(og[:, ls])))
    o_hg = jnp.concatenate(parts, axis=-1)
    hones = hones_ref[...]
    y = y_ref[0, 0] + y_ref[1, 0]
    mean = _dot_exact_rhs(y, hones) * (1.0 / RW_HEAD)
    yc = y - mean
    var = _dot_exact_rhs(yc * yc, hones) * (1.0 / RW_HEAD)
    yn = yc * lax.rsqrt(var + RW_GN_EPS) * gn_ref[0:1, :] + gn_ref[1:2, :]
    o_rw = (yn + bon_ref[0, 0] + bon_ref[1, 0]) * g_ref[0, 0]
    gt = gt_ref[0]
    m = (_sigmoid(gt[:, :D]) * _bdot(o_hg, pa_ref[...])
         + _sigmoid(gt[:, D:]) * _bdot(o_rw, pb_ref[...]))
    yx = _bdot(m, wo_ref[...])
    gate = mod_ref[0, :, 2 * D:3 * D]
    o_ref[0] = x_ref[0] + gate * _rms(yx, ng_ref[...])


def _merge(ohg, p_hg, y, bon, g, p_gt, xcat, modtab, hgn, gn, ng, hones, pa, pb, wo, t0):
    B, TC, D = xcat.shape
    nt = TC // TILE - t0
    full = lambda a: pl.BlockSpec(a.shape, lambda b, t: (0,) * a.ndim)
    two = pl.BlockSpec((2, 1, TILE, HG_WIDTH), lambda b, t: (0, b, t + t0, 0))
    return pl.pallas_call(
        _merge_kernel,
        grid=(B, nt),
        in_specs=[two,
                  pl.BlockSpec((1, TILE, HG_WIDTH), lambda b, t: (b, t + t0, 4)),
                  two, two,
                  pl.BlockSpec((1, 1, TILE, RW_WIDTH), lambda b, t: (0, b, t + t0, 0)),
                  pl.BlockSpec((1, TILE, 2 * D), lambda b, t: (b, t + t0, 0)),
                  pl.BlockSpec((1, TILE, D), lambda b, t: (b, t + t0, 0)),
                  pl.BlockSpec((1, 1, modtab.shape[-1]),
                               lambda b, t: (2 * b + jnp.minimum(t + t0, 1), 0, 0)),
                  full(hgn), full(gn), full(ng), full(hones), full(pa), full(pb), full(wo)],
        out_specs=pl.BlockSpec((1, TILE, D), lambda b, t: (b, t, 0)),
        out_shape=jax.ShapeDtypeStruct((B, nt * TILE, D), F32),
        compiler_params=_params("arbitrary", "arbitrary"),
        name="gated_merge",
    )(ohg, p_hg, y, bon, g, p_gt, xcat, modtab, hgn, gn, ng, hones, pa, pb, wo)


def _ffn_kernel(x_ref, mod_ref, g_ref, w1_ref, w2_ref, o_ref):
    D = x_ref.shape[-1]
    F = w2_ref.shape[0]
    x = x_ref[0]
    h = _rms(x, g_ref[0:1, :]) * (1.0 + mod_ref[0, :, 4 * D:5 * D]) + mod_ref[0, :, 3 * D:4 * D]
    hb = h.astype(BF16)
    gt = jnp.dot(hb, w1_ref[:, 0:F], preferred_element_type=F32)
    up = jnp.dot(hb, w1_ref[:, F:2 * F], preferred_element_type=F32)
    act = (gt * _sigmoid(gt) * up).astype(BF16)
    y = jnp.dot(act, w2_ref[...], preferred_element_type=F32)
    o_ref[0] = x + mod_ref[0, :, 5 * D:6 * D] * _rms(y, g_ref[1:2, :])


def _ffn(x, modtab, g, w1, w2, t0):
    B, TX, D = x.shape
    nt = TX // TILE
    full = lambda a: pl.BlockSpec(a.shape, lambda b, t: (0,) * a.ndim)
    return pl.pallas_call(
        _ffn_kernel,
        grid=(B, nt),
        in_specs=[pl.BlockSpec((1, TILE, D), lambda b, t: (b, t, 0)),
                  pl.BlockSpec((1, 1, modtab.shape[-1]),
                               lambda b, t: (2 * b + jnp.minimum(t + t0, 1), 0, 0)),
                  full(g), full(w1), full(w2)],
        out_specs=pl.BlockSpec((1, TILE, D), lambda b, t: (b, t, 0)),
        out_shape=jax.ShapeDtypeStruct((B, TX, D), F32),
        compiler_params=_params("arbitrary", "arbitrary"),
        name="swiglu_ffn",
    )(x, modtab, g, w1, w2)


def kernel(x, c, ctx, c_ctx, ada_w, ada_b, norm_g, w_in, hg_lb_logits, hg_norm_g, rw_mu, rw_w0, rw_w2,
           rw_a0, rw_a2, rw_g2, rw_kk, rw_ka, rw_rk, rw_gn_g, rw_gn_b, proj_a, proj_b, w_out, ffn_w1,
           ffn_w2):
    B, T, D = x.shape
    LCTX = ctx.shape[1]
    L = ada_w.shape[0]
    assert T % TILE == 0 and LCTX == TILE and TILE % GRID_W == 0 and T // TILE >= 2
    nctx = LCTX // TILE

    rows = -(-(B + 1) // 8) * 8
    cond = jnp.zeros((rows, D), F32).at[:B].set(c).at[B].set(c_ctx)
    mods = _mods(cond, ada_w, ada_b)

    lb_cum = jnp.cumsum(jax.nn.softmax(hg_lb_logits.astype(F32), axis=0), axis=0)
    hg_lb = lb_cum - lb_cum[:1]
    lbtab = jnp.stack([jnp.log(hg_lb), jnp.log1p(-hg_lb), 1.0 - hg_lb], axis=2)

    stack = jnp.asarray(_hgrn_stack(), BF16)
    tri = jnp.asarray(_rw_tri(), BF16)
    hones = jnp.asarray(_head_ones(), BF16)

    zeros = jnp.zeros_like(rw_w2[:, 0])
    w2cat = jnp.stack([jnp.concatenate([rw_w2[:, 0], zeros], axis=1),
                       jnp.concatenate([zeros, rw_w2[:, 1]], axis=1)], axis=1).astype(BF16)
    a2cat = jnp.stack([jnp.concatenate([rw_a2[:, 0], zeros], axis=1),
                       jnp.concatenate([zeros, rw_a2[:, 1]], axis=1)], axis=1).astype(BF16)

    xcat = jnp.concatenate([ctx, x], axis=1)
    for l in range(L):
        last = l == L - 1
        lat = mods[l, :B]
        ctxm = jnp.broadcast_to(mods[l, B], lat.shape)
        modtab = jnp.stack([ctxm, lat], axis=1).reshape(2 * B, 1, 6 * D)
        wl = w_in[l].astype(BF16)
        p_hg, p_rw, p_gt = _inproj(xcat, modtab, norm_g[l, 0:1], wl[:, :HG_COLS],
                                   wl[:, HG_COLS:HG_COLS + RW_COLS], wl[:, HG_COLS + RW_COLS:])
        ohg = _hgrn(p_hg, lbtab[l], stack, nctx)
        vecs = jnp.stack([rw_kk[l], rw_ka[l], rw_rk[l].reshape(-1)], axis=0)
        dvecs = jnp.stack([rw_w0[l], rw_a0[l]], axis=1)
        y, bon, g = _rwkv(p_rw, rw_mu[l][None, :], vecs, dvecs, w2cat[l], a2cat[l],
                          rw_g2[l].astype(BF16), hones, tri, nctx)
        t0 = nctx if last else 0
        xm = _merge(ohg, p_hg, y, bon, g, p_gt, xcat, modtab, hg_norm_g[l][None, :],
                    jnp.stack([rw_gn_g[l], rw_gn_b[l]], axis=0), norm_g[l, 1:2], hones,
                    proj_a[l].astype(BF16), proj_b[l].astype(BF16), w_out[l].astype(BF16), t0)
        xcat = _ffn(xm, modtab, norm_g[l, 2:4], ffn_w1[l].astype(BF16), ffn_w2[l].astype(BF16), t0)
    return xcat
```

```python
import functools

import numpy as np
import jax
import jax.numpy as jnp
from jax import lax
from jax.experimental import pallas as pl
from jax.experimental.pallas import tpu as pltpu

F32 = jnp.float32
BF16 = jnp.bfloat16

NORM_EPS = 1e-6
GRID_W = 64
HG_HEADS = 4
HG_HEAD = 128
HG_WIDTH = HG_HEADS * HG_HEAD
RW_HEADS = 8
RW_HEAD = 64
RW_WIDTH = RW_HEADS * RW_HEAD
RW_LORA = 64
RW_GATE_LORA = 128
RW_GN_EPS = 64e-5
RW_COLS = 3 * RW_WIDTH + 4 * RW_LORA + RW_GATE_LORA
HG_COLS = 5 * HG_WIDTH

TILE = 256
CHUNK = 64
LEVELS = 6
HG_MM_LEVELS = 3
HG_FILL = 2
PAIR = 2 * RW_HEAD
VMEM_LIMIT = 56 * 1024 * 1024


def _bdot(a, b):
    return jnp.dot(a.astype(BF16), b.astype(BF16), preferred_element_type=F32)


def _bdot_nt(a, b):
    return lax.dot_general(a.astype(BF16), b.astype(BF16), (((1,), (1,)), ((), ())),
                           preferred_element_type=F32)


def _bdot_tn(a, b):
    return lax.dot_general(a.astype(BF16), b.astype(BF16), (((0,), (0,)), ((), ())),
                           preferred_element_type=F32)


def _split3(x):
    hi = x.astype(BF16)
    r1 = x - hi.astype(F32)
    mid = r1.astype(BF16)
    lo = (r1 - mid.astype(F32)).astype(BF16)
    return hi, mid, lo


def _dot_exact_lhs(m01, x, terms=2):
    d = lambda t: jnp.dot(m01, t, preferred_element_type=F32)
    return sum(d(t) for t in _split3(x)[:terms])


def _dot_exact_rhs(x, m01, terms=2):
    d = lambda t: jnp.dot(t, m01, preferred_element_type=F32)
    return sum(d(t) for t in _split3(x)[:terms])


def _rms(x, g):
    ms = jnp.mean(x * x, axis=-1, keepdims=True)
    return x * lax.rsqrt(ms + NORM_EPS) * g


def _sigmoid(x):
    return 0.5 * jnp.tanh(0.5 * x) + 0.5


def _params(*sem):
    return pltpu.CompilerParams(dimension_semantics=sem, vmem_limit_bytes=VMEM_LIMIT)


def _mods_kernel(c_ref, w_ref, b_ref, o_ref):
    c = c_ref[...]
    s = c * _sigmoid(c)
    sh, sm, sl = _split3(s)
    wh, wm, wl = _split3(w_ref[0])
    d = lambda a, b: jnp.dot(a, b, preferred_element_type=F32)
    acc = d(sh, wh) + d(sh, wm) + d(sm, wh) + d(sh, wl) + d(sl, wh) + d(sm, wm)
    o_ref[0] = acc + b_ref[0]


def _mods(cond, ada_w, ada_b):
    L, D, N = ada_w.shape
    rows = cond.shape[0]
    tn = 1536
    return pl.pallas_call(
        _mods_kernel,
        grid=(L, N // tn),
        in_specs=[pl.BlockSpec((rows, D), lambda l, n: (0, 0)),
                  pl.BlockSpec((1, D, tn), lambda l, n: (l, 0, n)),
                  pl.BlockSpec((1, 1, tn), lambda l, n: (l, 0, n))],
        out_specs=pl.BlockSpec((1, rows, tn), lambda l, n: (l, 0, n)),
        out_shape=jax.ShapeDtypeStruct((L, rows, N), F32),
        compiler_params=_params("arbitrary", "arbitrary"),
        name="adaln_mods",
    )(cond, ada_w, ada_b.reshape(L, 1, N))


def _inproj_kernel(x_ref, mod_ref, g_ref, whg_ref, wrw_ref, wgt_ref, ohg_ref, orw_ref, ogt_ref):
    D = x_ref.shape[-1]
    y = _rms(x_ref[0], g_ref[...])
    shift = mod_ref[0, :, 0:D]
    scale = mod_ref[0, :, D:2 * D]
    h = (y * (1.0 + scale) + shift).astype(BF16)
    ohg_ref[0] = jnp.dot(h, whg_ref[...], preferred_element_type=F32)
    orw_ref[0] = jnp.dot(h, wrw_ref[...], preferred_element_type=F32)
    ogt_ref[0] = jnp.dot(h, wgt_ref[...], preferred_element_type=F32)


def _inproj(xcat, modtab, g, whg, wrw, wgt):
    B, TC, D = xcat.shape
    nt = TC // TILE
    full = lambda a: pl.BlockSpec(a.shape, lambda b, t: (0,) * a.ndim)
    outs = [jax.ShapeDtypeStruct((B, TC, w.shape[1]), F32) for w in (whg, wrw, wgt)]
    return pl.pallas_call(
        _inproj_kernel,
        grid=(B, nt),
        in_specs=[pl.BlockSpec((1, TILE, D), lambda b, t: (b, t, 0)),
                  pl.BlockSpec((1, 1, modtab.shape[-1]), lambda b, t: (2 * b + jnp.minimum(t, 1), 0, 0)),
                  full(g), full(whg), full(wrw), full(wgt)],
        out_specs=[pl.BlockSpec((1, TILE, o.shape[-1]), lambda b, t: (b, t, 0)) for o in outs],
        out_shape=outs,
        compiler_params=_params("arbitrary", "arbitrary"),
        name="in_proj",
    )(xcat, modtab, g, whg, wrw, wgt)


def _scan_tile(d, s, nt, nctx):
    bwd = jnp.where(s < nctx, nctx - 1 - s, nt - 1 - (s - nctx))
    return jnp.where(d == 0, s, bwd)


def _hgrn_stack():
    c = CHUNK
    out = np.zeros((2, (HG_MM_LEVELS + 1) * c, c), np.float32)
    for d in range(2):
        pi = np.arange(c) if d == 0 else c - 1 - np.arange(c)
        out[d, :c] = pi[None, :] <= pi[:, None]
        for lv in range(1, HG_MM_LEVELS + 1):
            m = 1 << lv
            mid = (pi // m) * m + m // 2
            late = (pi % m) >= m // 2
            sum_late = (pi[None, :] >= mid[:, None]) & (pi[None, :] <= pi[:, None])
            sum_early = (pi[None, :] > pi[:, None]) & (pi[None, :] <= mid[:, None] - 1)
            out[d, lv * c:(lv + 1) * c] = np.where(late[:, None], sum_late, sum_early)
    return out


def _hgrn_steps(q_ref, f_ref, i_ref, lb_ref, stack_ref, o_ref, st_ref, d, s):
    C = CHUNK
    H = HG_HEAD

    @pl.when(s == 0)
    def _():
        st_ref[...] = jnp.zeros_like(st_ref)

    log_lb = lb_ref[0, 0:1, :]
    log1m_lb = lb_ref[0, 1:2, :]
    one_m_lb = lb_ref[0, 2:3, :]
    stack = stack_ref[0]

    row = lax.broadcasted_iota(jnp.int32, (C, HG_WIDTH), 0)
    pi = jnp.where(d == 0, row, C - 1 - row)
    ii = lax.broadcasted_iota(jnp.int32, (C, C), 0)
    jj = lax.broadcasted_iota(jnp.int32, (C, C), 1)
    nchunk = TILE // C
    hsl = [slice(h * H, (h + 1) * H) for h in range(HG_HEADS)]
    ntdot = lambda a, b: lax.dot_general(a, b, (((1,), (1,)), ((), ())), preferred_element_type=F32)

    def chunk_local(cj, out):
        r0 = pl.multiple_of(cj * C, C)
        xf = f_ref[0, pl.ds(r0, C), :]
        lsig = jnp.minimum(xf, 0.0) - jnp.log(1.0 + jnp.exp(-jnp.abs(xf)))
        t2 = log1m_lb + lsig
        mx = jnp.maximum(log_lb, t2)
        logf = mx + jnp.log(1.0 + jnp.exp(-jnp.abs(log_lb - t2)))
        kin = one_m_lb * _sigmoid(-xf)
        qr = q_ref[0, pl.ds(r0, C), :]
        q = qr * _sigmoid(qr) * (HG_HEAD ** -0.5)
        vb = i_ref[0, pl.ds(r0, C), :].astype(BF16)

        e_all = _dot_exact_lhs(stack, logf)
        cs = e_all[0:C]
        tot = jnp.sum(logf, axis=0, keepdims=True)
        qb = q.astype(BF16)
        kb = kin.astype(BF16)
        scores = [jnp.where(ii == jj, ntdot(qb[:, ls], kb[:, ls]), 0.0) for ls in hsl]
        yield
        for lv in range(1, LEVELS + 1):
            m = 1 << lv
            late = ((pi >> (lv - 1)) & 1) == 1
            if lv <= HG_MM_LEVELS:
                e = e_all[lv * C:(lv + 1) * C]
            else:
                cmid = jnp.concatenate(
                    [jnp.broadcast_to(jnp.where(d == 0, cs[b * m + m // 2 - 1:b * m + m // 2],
                                                cs[b * m + m // 2:b * m + m // 2 + 1]),
                                      (m, HG_WIDTH)) for b in range(C // m)], axis=0)
                e = jnp.where(late, cs - cmid, cmid - cs)
            ex = jnp.exp(e)
            qm = jnp.where(late, q * ex, 0.0).astype(BF16)
            km = jnp.where(late, 0.0, kin * ex).astype(BF16)
            sc = [ntdot(qm[:, ls], km[:, ls]) for ls in hsl]
            if lv < LEVELS:
                same = (ii >> lv) == (jj >> lv)
                sc = [jnp.where(same, x, 0.0) for x in sc]
            scores = [a + b for a, b in zip(scores, sc)]
            yield
        q_in = (q * jnp.exp(cs)).astype(BF16)
        k_end = (kin * jnp.exp(tot - cs)).astype(BF16)
        o_loc = [jnp.dot(scores[h].astype(BF16), vb[:, ls], preferred_element_type=F32)
                 for h, ls in enumerate(hsl)]
        kv = [lax.dot_general(vb[:, ls], k_end[:, ls], (((0,), (0,)), ((), ())),
                              preferred_element_type=F32) for ls in hsl]
        out.append((r0, q_in, o_loc, kv, jnp.exp(tot)))
        yield

    loc = []
    for step in range(nchunk):
        yield from chunk_local(jnp.where(d == 0, step, nchunk - 1 - step), loc)
    st = [st_ref[h] for h in range(HG_HEADS)]
    for r0, q_in, o_loc, kv, dec_end in loc:
        for h, ls in enumerate(hsl):
            o_ref[0, 0, pl.ds(r0, C), ls] = ntdot(q_in[:, ls], st[h].astype(BF16)) + o_loc[h]
        st = [st[h] * dec_end[:, ls] + kv[h] for h, ls in enumerate(hsl)]
        yield
    for h in range(HG_HEADS):
        st_ref[h] = st[h]


def _rw_tri():
    out = np.zeros((2, TILE, TILE), np.float32)
    idx = np.arange(TILE)
    same = (idx[:, None] // CHUNK) == (idx[None, :] // CHUNK)
    out[0] = same & (idx[None, :] <= idx[:, None])
    out[1] = same & (idx[None, :] >= idx[:, None])
    return out


def _head_ones():
    idx = np.arange(RW_WIDTH)
    return ((idx[:, None] // RW_HEAD) == (idx[None, :] // RW_HEAD)).astype(np.float32)


def _rwkv_pre_kernel(cur_ref, up_ref, dn_ref, mu_ref, vec_ref, dvec_ref, w2_ref, a2_ref, g2_ref,
                     hones_ref, tri_ref, at_ref, rt_ref, bt_ref, kt_ref, v_ref, gam_ref, bon_ref, g_ref,
                     *, nt, nctx):
    t = pl.program_id(1)
    C = CHUNK
    W = RW_WIDTH

    p = cur_ref[0]
    is_ctx = t < nctx
    rowi = lax.broadcasted_iota(jnp.int32, p.shape, 0)
    lane = lax.broadcasted_iota(jnp.int32, p.shape, 1)
    prev1 = pltpu.roll(p, 1, 0)
    next1 = pltpu.roll(p, TILE - 1, 0)
    seg = jnp.where(is_ctx, TILE - 1, GRID_W - 1)
    prev1 = jnp.where((rowi & seg) == 0, 0.0, prev1)
    next1 = jnp.where((rowi & seg) == seg, 0.0, next1)
    up_ok = t > nctx
    dn_ok = jnp.logical_and(t >= nctx, t < nt - 1)
    up = jnp.concatenate([jnp.where(up_ok, up_ref[0], 0.0), p[:TILE - GRID_W]], axis=0)
    dn = jnp.concatenate([p[GRID_W:], jnp.where(dn_ok, dn_ref[0], 0.0)], axis=0)
    cls = lane & 3
    sh_even = jnp.where(cls == 0, prev1, jnp.where(is_ctx, prev1, up))
    sh_odd = jnp.where(cls == 1, next1, jnp.where(is_ctx, next1, dn))
    shifted = jnp.where((cls & 1) == 0, sh_even, sh_odd)
    pf = p + mu_ref[...] * (shifted - p)

    r = pf[:, 0:W]
    k = pf[:, W:2 * W]
    v = pf[:, 2 * W:3 * W]
    o3 = 3 * W
    wd = pf[:, o3:o3 + 2 * RW_LORA]
    ad = pf[:, o3 + 2 * RW_LORA:o3 + 4 * RW_LORA]
    gd = pf[:, o3 + 4 * RW_LORA:o3 + 4 * RW_LORA + RW_GATE_LORA]

    k_k = vec_ref[0:1, :]
    k_a = vec_ref[1:2, :]
    r_k = vec_ref[2:3, :]
    hones = hones_ref[...]

    kkr = k * k_k
    ss = _dot_exact_rhs(kkr * kkr, hones)
    kk = kkr * lax.rsqrt(jnp.maximum(ss, 1e-24))
    g_ref[0] = _bdot(_sigmoid(gd), g2_ref[...])
    v_ref[0] = v.astype(BF16)
    twd = jnp.tanh(wd).astype(BF16)
    adb = ad.astype(BF16)
    ksum = None
    for dd in range(2):
        lw = dvec_ref[dd, 0:1, :] + jnp.dot(twd, w2_ref[dd], preferred_element_type=F32)
        ld = -float(np.exp(-0.5)) * _sigmoid(lw)
        a = _sigmoid(dvec_ref[dd, 1:2, :] + jnp.dot(adb, a2_ref[dd], preferred_element_type=F32))
        kdir = k * (1.0 + (a - 1.0) * k_a)
        ksum = kdir if ksum is None else ksum + kdir
        cs = _dot_exact_lhs(tri_ref[dd], ld)
        einv = jnp.exp(-cs)
        at_ref[dd, 0] = (-kk * jnp.exp(cs - ld)).astype(BF16)
        rt_ref[dd, 0] = (r * jnp.exp(cs)).astype(BF16)
        bt_ref[dd, 0] = (kk * a * einv).astype(BF16)
        kt_ref[dd, 0] = (kdir * einv).astype(BF16)
        for c in range(TILE // C):
            tot = jnp.sum(ld[c * C:(c + 1) * C], axis=0, keepdims=True)
            gam_ref[dd, 0, c * 8:(c + 1) * 8, :] = jnp.broadcast_to(jnp.exp(tot), (8, W))
    bon_ref[0] = _dot_exact_rhs(r * ksum * r_k, hones) * v


def _mixer_kernel(q_ref, f_ref, i_ref, lb_ref, stack_ref, at_ref, rt_ref, bt_ref, kt_ref, v_ref,
                  gam_ref, o_ref, y_ref, st_ref, s_ref, *, nt, nctx):
    d = pl.program_id(1)
    s = pl.program_id(2)
    C = CHUNK
    nchunk = TILE // C

    hgrn = _hgrn_steps(q_ref, f_ref, i_ref, lb_ref, stack_ref, o_ref, st_ref, d, s)

    def fill(n):
        for _ in range(n):
            next(hgrn, None)

    @pl.when(s == 0)
    def _():
        s_ref[...] = jnp.zeros_like(s_ref)

    sgn = jnp.where(d == 0, 1, -1)
    pi_ = lax.broadcasted_iota(jnp.int32, (PAIR, PAIR), 0)
    pj_ = lax.broadcasted_iota(jnp.int32, (PAIR, PAIR), 1)
    same_head = (pi_ < RW_HEAD) == (pj_ < RW_HEAD)
    eye_pair = (pi_ == pj_).astype(F32)
    head0 = lax.broadcasted_iota(jnp.int32, (C, PAIR), 1) < RW_HEAD

    ii4 = lax.broadcasted_iota(jnp.int32, (C, 2 * PAIR), 0)
    jj4 = lax.broadcasted_iota(jnp.int32, (C, 2 * PAIR), 1) & (C - 1)
    before4 = (ii4 - jj4) * sgn > 0
    upto4 = (ii4 - jj4) * sgn >= 0
    eye2 = (lax.broadcasted_iota(jnp.int32, (C, PAIR), 0)
            == (lax.broadcasted_iota(jnp.int32, (C, PAIR), 1) & (C - 1))).astype(F32)
    npair = RW_HEADS // 2
    lsl = [slice(pr * PAIR, (pr + 1) * PAIR) for pr in range(npair)]
    pairs = [(u, pr) for u in range(nchunk) for pr in range(npair)]
    fdot = lambda a, b: jnp.dot(a, b, preferred_element_type=F32)
    cat0 = lambda xs: jnp.concatenate(xs, axis=0)

    def bd(x):
        x = x.astype(F32)
        return cat0([jnp.where(head0, x, 0.0), jnp.where(head0, 0.0, x)]).astype(BF16)

    rows, gam_c = [], []
    for u in range(nchunk):
        cj = jnp.where(d == 0, u, nchunk - 1 - u)
        rows.append(pl.ds(pl.multiple_of(cj * C, C), C))
        gam_c.append(gam_ref[0, 0, pl.ds(pl.multiple_of(cj * 8, 8), 8), :])
    at = {(u, pr): at_ref[0, 0, rows[u], lsl[pr]] for u, pr in pairs}
    rt = {(u, pr): rt_ref[0, 0, rows[u], lsl[pr]] for u, pr in pairs}
    bt = {(u, pr): bt_ref[0, 0, rows[u], lsl[pr]] for u, pr in pairs}
    kt = {(u, pr): kt_ref[0, 0, rows[u], lsl[pr]] for u, pr in pairs}
    vv = {(u, pr): v_ref[0, rows[u], lsl[pr]] for u, pr in pairs}
    bk = {pq: cat0([bt[pq], kt[pq]]) for pq in pairs}
    bdv = {pq: bd(vv[pq]) for pq in pairs}
    top, bot = {}, {}
    for pq in pairs:
        lhs = cat0([at[pq], rt[pq]])
        x = _bdot_nt(lhs, cat0([bd(bt[pq]), bd(kt[pq])]))
        top[pq] = jnp.where(before4, x[:C], 0.0)
        bot[pq] = jnp.where(upto4, x[C:], 0.0).astype(BF16)
    fill(HG_FILL)
    nf = {pq: top[pq][:, :PAIR] for pq in pairs}
    tinv = {pq: eye2 + nf[pq] for pq in pairs}
    pw = {pq: fdot(nf[pq].astype(BF16), bd(nf[pq])) for pq in pairs}
    fill(HG_FILL)
    for _ in range(LEVELS - 2):
        res = {pq: fdot(cat0([tinv[pq], pw[pq]]).astype(BF16), bd(pw[pq])) for pq in pairs}
        tinv = {pq: tinv[pq] + res[pq][:C] for pq in pairs}
        pw = {pq: res[pq][C:] for pq in pairs}
        fill(HG_FILL)
    tinv = {pq: tinv[pq] + fdot(tinv[pq].astype(BF16), bd(pw[pq])) for pq in pairs}
    tb = {pq: tinv[pq].astype(BF16) for pq in pairs}
    fill(HG_FILL)
    w = {pq: fdot(top[pq][:, PAIR:].astype(BF16), bdv[pq]) for pq in pairs}
    ahf = {pq: fdot(tb[pq], bd(at[pq])) for pq in pairs}
    fill(HG_FILL)
    uvf = {pq: fdot(tb[pq], bd(w[pq])) for pq in pairs}
    fill(HG_FILL)
    rhat = {pq: rt[pq] + fdot(bot[pq][:, :PAIR], bd(ahf[pq])) for pq in pairs}
    yloc = {pq: fdot(bot[pq], cat0([bd(uvf[pq]), bdv[pq]])) for pq in pairs}
    fill(HG_FILL)
    gam = {(u, pr): gam_c[u][0:1, lsl[pr]] for u, pr in pairs}
    mm = {pq: jnp.where(same_head, eye_pair + _bdot_tn(ahf[pq], bk[pq][:C]), 0.0) * gam[pq]
          for pq in pairs}
    fill(HG_FILL)
    nn = {pq: jnp.where(same_head, _bdot_tn(cat0([uvf[pq].astype(BF16), vv[pq]]), bk[pq]), 0.0)
          * gam[pq] for pq in pairs}
    fill(HG_FILL)
    st = [s_ref[pr] for pr in range(npair)]
    for u in range(nchunk):
        stb = [x.astype(BF16) for x in st]
        for pr in range(npair):
            y_ref[0, 0, rows[u], lsl[pr]] = _bdot_nt(rhat[u, pr], stb[pr]) + yloc[u, pr]
        st = [_bdot(stb[pr], mm[u, pr]) + nn[u, pr] for pr in range(npair)]
        fill(HG_FILL)
    for pr in range(npair):
        s_ref[pr] = st[pr]
    for _ in hgrn:
        pass


def _rwkv_pre(p_rw, mu, vecs, dvecs, w2cat, a2cat, g2, hones, tri, nctx):
    B, TC, _ = p_rw.shape
    nt = TC // TILE
    hb = TILE // GRID_W
    nhb = TC // GRID_W
    grows = 8 * (TILE // CHUNK)
    full = lambda a: pl.BlockSpec(a.shape, lambda b, t: (0,) * a.ndim)
    both = pl.BlockSpec((2, 1, TILE, RW_WIDTH), lambda b, t: (0, b, t, 0))
    one = pl.BlockSpec((1, TILE, RW_WIDTH), lambda b, t: (b, t, 0))
    dir_bf = jax.ShapeDtypeStruct((2, B, TC, RW_WIDTH), BF16)
    return pl.pallas_call(
        functools.partial(_rwkv_pre_kernel, nt=nt, nctx=nctx),
        grid=(B, nt),
        in_specs=[pl.BlockSpec((1, TILE, RW_COLS), lambda b, t: (b, t, 0)),
                  pl.BlockSpec((1, GRID_W, RW_COLS), lambda b, t: (b, jnp.maximum(t * hb - 1, 0), 0)),
                  pl.BlockSpec((1, GRID_W, RW_COLS),
                               lambda b, t: (b, jnp.minimum((t + 1) * hb, nhb - 1), 0)),
                  full(mu), full(vecs), full(dvecs), full(w2cat), full(a2cat), full(g2),
                  full(hones), full(tri)],
        out_specs=[both, both, both, both, one,
                   pl.BlockSpec((2, 1, grows, RW_WIDTH), lambda b, t: (0, b, t, 0)), one, one],
        out_shape=[dir_bf, dir_bf, dir_bf, dir_bf, jax.ShapeDtypeStruct((B, TC, RW_WIDTH), BF16),
                   jax.ShapeDtypeStruct((2, B, nt * grows, RW_WIDTH), F32),
                   jax.ShapeDtypeStruct((B, TC, RW_WIDTH), F32),
                   jax.ShapeDtypeStruct((B, TC, RW_WIDTH), F32)],
        compiler_params=_params("arbitrary", "arbitrary"),
        name="rwkv7_pre",
    )(p_rw, p_rw, p_rw, mu, vecs, dvecs, w2cat, a2cat, g2, hones, tri)


def _mixer(p_hg, lbtab, stack, at, rt, bt, kt, v, gam, nctx):
    _, B, TC, _ = at.shape
    nt = TC // TILE
    grows = 8 * (TILE // CHUNK)
    tile = lambda b, d, s: _scan_tile(d, s, nt, nctx)
    per_dir = lambda a: pl.BlockSpec((1,) + a.shape[1:], lambda b, d, s: (d,) + (0,) * (a.ndim - 1))
    hg_col = lambda col: pl.BlockSpec((1, TILE, HG_WIDTH), lambda b, d, s: (b, tile(b, d, s), col))
    dir_tile = pl.BlockSpec((1, 1, TILE, RW_WIDTH), lambda b, d, s: (d, b, tile(b, d, s), 0))
    out_shape = jax.ShapeDtypeStruct((2, B, TC, RW_WIDTH), F32)
    assert HG_WIDTH == RW_WIDTH
    return pl.pallas_call(
        functools.partial(_mixer_kernel, nt=nt, nctx=nctx),
        grid=(B, 2, nt),
        in_specs=[hg_col(0),
                  pl.BlockSpec((1, TILE, HG_WIDTH), lambda b, d, s: (b, tile(b, d, s), 1 + d)),
                  hg_col(3), per_dir(lbtab), per_dir(stack),
                  dir_tile, dir_tile, dir_tile, dir_tile,
                  pl.BlockSpec((1, TILE, RW_WIDTH), lambda b, d, s: (b, tile(b, d, s), 0)),
                  pl.BlockSpec((1, 1, grows, RW_WIDTH), lambda b, d, s: (d, b, tile(b, d, s), 0))],
        out_specs=[dir_tile, dir_tile],
        out_shape=[out_shape, out_shape],
        scratch_shapes=[pltpu.VMEM((HG_HEADS, HG_HEAD, HG_HEAD), F32),
                        pltpu.VMEM((RW_HEADS // 2, PAIR, PAIR), F32)],
        compiler_params=_params("arbitrary", "arbitrary", "arbitrary"),
        name="mixer_scans",
    )(p_hg, p_hg, p_hg, lbtab, stack, at, rt, bt, kt, v, gam)


def _merge_kernel(ohg_ref, og_ref, y_ref, bon_ref, g_ref, gt_ref, x_ref, mod_ref,
                  hgn_ref, gn_ref, ng_ref, hones_ref, pa_ref, pb_ref, wo_ref, o_ref):
    D = x_ref.shape[-1]
    H = HG_HEAD
    oh = ohg_ref[0, 0] + ohg_ref[1, 0]
    og = og_ref[0]
    hgn = hgn_ref[...]
    parts = []
    for h in range(HG_HEADS):
        ls = slice(h * H, (h + 1) * H)
        parts.append(_rms(oh[:, ls], hgn) * (og[:, ls] * _sigmoid(og[:, ls])))
    o_hg = jnp.concatenate(parts, axis=-1)
    hones = hones_ref[...]
    y = y_ref[0, 0] + y_ref[1, 0]
    mean = _dot_exact_rhs(y, hones) * (1.0 / RW_HEAD)
    yc = y - mean
    var = _dot_exact_rhs(yc * yc, hones) * (1.0 / RW_HEAD)
    yn = yc * lax.rsqrt(var + RW_GN_EPS) * gn_ref[0:1, :] + gn_ref[1:2, :]
    o_rw = (yn + bon_ref[0]) * g_ref[0]
    gt = gt_ref[0]
    m = (_sigmoid(gt[:, :D]) * _bdot(o_hg, pa_ref[...])
         + _sigmoid(gt[:, D:]) * _bdot(o_rw, pb_ref[...]))
    yx = _bdot(m, wo_ref[...])
    gate = mod_ref[0, :, 2 * D:3 * D]
    o_ref[0] = x_ref[0] + gate * _rms(yx, ng_ref[...])


def _merge(ohg, p_hg, y, bon, g, p_gt, xcat, modtab, hgn, gn, ng, hones, pa, pb, wo, t0):
    B, TC, D = xcat.shape
    nt = TC // TILE - t0
    full = lambda a: pl.BlockSpec(a.shape, lambda b, t: (0,) * a.ndim)
    two = pl.BlockSpec((2, 1, TILE, HG_WIDTH), lambda b, t: (0, b, t + t0, 0))
    return pl.pallas_call(
        _merge_kernel,
        grid=(B, nt),
        in_specs=[two,
                  pl.BlockSpec((1, TILE, HG_WIDTH), lambda b, t: (b, t + t0, 4)),
                  two,
                  pl.BlockSpec((1, TILE, RW_WIDTH), lambda b, t: (b, t + t0, 0)),
                  pl.BlockSpec((1, TILE, RW_WIDTH), lambda b, t: (b, t + t0, 0)),
                  pl.BlockSpec((1, TILE, 2 * D), lambda b, t: (b, t + t0, 0)),
                  pl.BlockSpec((1, TILE, D), lambda b, t: (b, t + t0, 0)),
                  pl.BlockSpec((1, 1, modtab.shape[-1]),
                               lambda b, t: (2 * b + jnp.minimum(t + t0, 1), 0, 0)),
                  full(hgn), full(gn), full(ng), full(hones), full(pa), full(pb), full(wo)],
        out_specs=pl.BlockSpec((1, TILE, D), lambda b, t: (b, t, 0)),
        out_shape=jax.ShapeDtypeStruct((B, nt * TILE, D), F32),
        compiler_params=_params("arbitrary", "arbitrary"),
        name="gated_merge",
    )(ohg, p_hg, y, bon, g, p_gt, xcat, modtab, hgn, gn, ng, hones, pa, pb, wo)


def _ffn_kernel(x_ref, mod_ref, g_ref, w1_ref, w2_ref, o_ref):
    D = x_ref.shape[-1]
    F = w2_ref.shape[0]
    x = x_ref[0]
    h = _rms(x, g_ref[0:1, :]) * (1.0 + mod_ref[0, :, 4 * D:5 * D]) + mod_ref[0, :, 3 * D:4 * D]
    hb = h.astype(BF16)
    gt = jnp.dot(hb, w1_ref[:, 0:F], preferred_element_type=F32)
    up = jnp.dot(hb, w1_ref[:, F:2 * F], preferred_element_type=F32)
    act = (gt * _sigmoid(gt) * up).astype(BF16)
    y = jnp.dot(act, w2_ref[...], preferred_element_type=F32)
    o_ref[0] = x + mod_ref[0, :, 5 * D:6 * D] * _rms(y, g_ref[1:2, :])


def _ffn(x, modtab, g, w1, w2, t0):
    B, TX, D = x.shape
    nt = TX // TILE
    full = lambda a: pl.BlockSpec(a.shape, lambda b, t: (0,) * a.ndim)
    return pl.pallas_call(
        _ffn_kernel,
        grid=(B, nt),
        in_specs=[pl.BlockSpec((1, TILE, D), lambda b, t: (b, t, 0)),
                  pl.BlockSpec((1, 1, modtab.shape[-1]),
                               lambda b, t: (2 * b + jnp.minimum(t + t0, 1), 0, 0)),
                  full(g), full(w1), full(w2)],
        out_specs=pl.BlockSpec((1, TILE, D), lambda b, t: (b, t, 0)),
        out_shape=jax.ShapeDtypeStruct((B, TX, D), F32),
        compiler_params=_params("arbitrary", "arbitrary"),
        name="swiglu_ffn",
    )(x, modtab, g, w1, w2)


def kernel(x, c, ctx, c_ctx, ada_w, ada_b, norm_g, w_in, hg_lb_logits, hg_norm_g, rw_mu, rw_w0, rw_w2,
           rw_a0, rw_a2, rw_g2, rw_kk, rw_ka, rw_rk, rw_gn_g, rw_gn_b, proj_a, proj_b, w_out, ffn_w1,
           ffn_w2):
    B, T, D = x.shape
    LCTX = ctx.shape[1]
    L = ada_w.shape[0]
    assert T % TILE == 0 and LCTX == TILE and TILE % GRID_W == 0 and T // TILE >= 2
    nctx = LCTX // TILE

    rows = -(-(B + 1) // 8) * 8
    cond = jnp.zeros((rows, D), F32).at[:B].set(c).at[B].set(c_ctx)
    mods = _mods(cond, ada_w, ada_b)

    lb_cum = jnp.cumsum(jax.nn.softmax(hg_lb_logits.astype(F32), axis=0), axis=0)
    hg_lb = lb_cum - lb_cum[:1]
    lbtab = jnp.stack([jnp.log(hg_lb), jnp.log1p(-hg_lb), 1.0 - hg_lb], axis=2)

    stack = jnp.asarray(_hgrn_stack(), BF16)
    tri = jnp.asarray(_rw_tri(), BF16)
    hones = jnp.asarray(_head_ones(), BF16)

    zeros = jnp.zeros_like(rw_w2[:, 0])
    w2cat = jnp.stack([jnp.concatenate([rw_w2[:, 0], zeros], axis=1),
                       jnp.concatenate([zeros, rw_w2[:, 1]], axis=1)], axis=1).astype(BF16)
    a2cat = jnp.stack([jnp.concatenate([rw_a2[:, 0], zeros], axis=1),
                       jnp.concatenate([zeros, rw_a2[:, 1]], axis=1)], axis=1).astype(BF16)

    xcat = jnp.concatenate([ctx, x], axis=1)
    for l in range(L):
        last = l == L - 1
        lat = mods[l, :B]
        ctxm = jnp.broadcast_to(mods[l, B], lat.shape)
        modtab = jnp.stack([ctxm, lat], axis=1).reshape(2 * B, 1, 6 * D)
        wl = w_in[l].astype(BF16)
        p_hg, p_rw, p_gt = _inproj(xcat, modtab, norm_g[l, 0:1], wl[:, :HG_COLS],
                                   wl[:, HG_COLS:HG_COLS + RW_COLS], wl[:, HG_COLS + RW_COLS:])
        vecs = jnp.stack([rw_kk[l], rw_ka[l], rw_rk[l].reshape(-1)], axis=0)
        dvecs = jnp.stack([rw_w0[l], rw_a0[l]], axis=1)
        at, rt, bt, kt, v, gam, bon, g = _rwkv_pre(p_rw, rw_mu[l][None, :], vecs, dvecs, w2cat[l],
                                                   a2cat[l], rw_g2[l].astype(BF16), hones, tri, nctx)
        ohg, y = _mixer(p_hg, lbtab[l], stack, at, rt, bt, kt, v, gam, nctx)
        t0 = nctx if last else 0
        xm = _merge(ohg, p_hg, y, bon, g, p_gt, xcat, modtab, hg_norm_g[l][None, :],
                    jnp.stack([rw_gn_g[l], rw_gn_b[l]], axis=0), norm_g[l, 1:2], hones,
                    proj_a[l].astype(BF16), proj_b[l].astype(BF16), w_out[l].astype(BF16), t0)
        xcat = _ffn(xm, modtab, norm_g[l, 2:4], ffn_w1[l].astype(BF16), ffn_w2[l].astype(BF16), t0)
    return xcat
```

```python
import functools

import numpy as np
import jax
import jax.numpy as jnp
from jax import lax
from jax.experimental import pallas as pl
from jax.experimental.pallas import tpu as pltpu

F32 = jnp.float32
BF16 = jnp.bfloat16

NORM_EPS = 1e-6
GRID_W = 64
HG_HEADS = 4
HG_HEAD = 128
HG_WIDTH = HG_HEADS * HG_HEAD
RW_HEADS = 8
RW_HEAD = 64
RW_WIDTH = RW_HEADS * RW_HEAD
RW_LORA = 64
RW_GATE_LORA = 128
RW_GN_EPS = 64e-5
RW_COLS = 3 * RW_WIDTH + 4 * RW_LORA + RW_GATE_LORA
HG_COLS = 5 * HG_WIDTH

TILE = 256
CHUNK = 64
LEVELS = 6
HG_MM_LEVELS = 3
HG_FILL = 3
PAIR = 2 * RW_HEAD
VMEM_LIMIT = 56 * 1024 * 1024


def _bdot(a, b):
    return jnp.dot(a.astype(BF16), b.astype(BF16), preferred_element_type=F32)


def _bdot_nt(a, b):
    return lax.dot_general(a.astype(BF16), b.astype(BF16), (((1,), (1,)), ((), ())),
                           preferred_element_type=F32)


def _bdot_tn(a, b):
    return lax.dot_general(a.astype(BF16), b.astype(BF16), (((0,), (0,)), ((), ())),
                           preferred_element_type=F32)


def _split3(x):
    hi = x.astype(BF16)
    r1 = x - hi.astype(F32)
    mid = r1.astype(BF16)
    lo = (r1 - mid.astype(F32)).astype(BF16)
    return hi, mid, lo


def _dot_exact_lhs(m01, x, terms=2):
    d = lambda t: jnp.dot(m01, t, preferred_element_type=F32)
    return sum(d(t) for t in _split3(x)[:terms])


def _dot_exact_rhs(x, m01, terms=2):
    d = lambda t: jnp.dot(t, m01, preferred_element_type=F32)
    return sum(d(t) for t in _split3(x)[:terms])


def _rms(x, g):
    ms = jnp.mean(x * x, axis=-1, keepdims=True)
    return x * lax.rsqrt(ms + NORM_EPS) * g


def _sigmoid(x):
    return 0.5 * jnp.tanh(0.5 * x) + 0.5


def _params(*sem):
    return pltpu.CompilerParams(dimension_semantics=sem, vmem_limit_bytes=VMEM_LIMIT)


def _mods_kernel(c_ref, w_ref, b_ref, o_ref):
    c = c_ref[...]
    s = c * _sigmoid(c)
    sh, sm, sl = _split3(s)
    wh, wm, wl = _split3(w_ref[0])
    d = lambda a, b: jnp.dot(a, b, preferred_element_type=F32)
    acc = d(sh, wh) + d(sh, wm) + d(sm, wh) + d(sh, wl) + d(sl, wh) + d(sm, wm)
    o_ref[0] = acc + b_ref[0]


def _mods(cond, ada_w, ada_b):
    L, D, N = ada_w.shape
    rows = cond.shape[0]
    tn = 1536
    return pl.pallas_call(
        _mods_kernel,
        grid=(L, N // tn),
        in_specs=[pl.BlockSpec((rows, D), lambda l, n: (0, 0)),
                  pl.BlockSpec((1, D, tn), lambda l, n: (l, 0, n)),
                  pl.BlockSpec((1, 1, tn), lambda l, n: (l, 0, n))],
        out_specs=pl.BlockSpec((1, rows, tn), lambda l, n: (l, 0, n)),
        out_shape=jax.ShapeDtypeStruct((L, rows, N), F32),
        compiler_params=_params("arbitrary", "arbitrary"),
        name="adaln_mods",
    )(cond, ada_w, ada_b.reshape(L, 1, N))


def _inproj_kernel(x_ref, mod_ref, g_ref, whg_ref, wrw_ref, wgt_ref, ohg_ref, orw_ref, ogt_ref):
    D = x_ref.shape[-1]
    y = _rms(x_ref[0], g_ref[...])
    shift = mod_ref[0, :, 0:D]
    scale = mod_ref[0, :, D:2 * D]
    h = (y * (1.0 + scale) + shift).astype(BF16)
    ohg_ref[0] = jnp.dot(h, whg_ref[...], preferred_element_type=F32)
    orw_ref[0] = jnp.dot(h, wrw_ref[...], preferred_element_type=F32)
    ogt_ref[0] = jnp.dot(h, wgt_ref[...], preferred_element_type=F32).astype(ogt_ref.dtype)


def _inproj(xcat, modtab, g, whg, wrw, wgt):
    B, TC, D = xcat.shape
    nt = TC // TILE
    full = lambda a: pl.BlockSpec(a.shape, lambda b, t: (0,) * a.ndim)
    outs = [jax.ShapeDtypeStruct((B, TC, w.shape[1]), dt) for w, dt in ((whg, F32), (wrw, F32), (wgt, BF16))]
    return pl.pallas_call(
        _inproj_kernel,
        grid=(B, nt),
        in_specs=[pl.BlockSpec((1, TILE, D), lambda b, t: (b, t, 0)),
                  pl.BlockSpec((1, 1, modtab.shape[-1]), lambda b, t: (2 * b + jnp.minimum(t, 1), 0, 0)),
                  full(g), full(whg), full(wrw), full(wgt)],
        out_specs=[pl.BlockSpec((1, TILE, o.shape[-1]), lambda b, t: (b, t, 0)) for o in outs],
        out_shape=outs,
        compiler_params=_params("arbitrary", "arbitrary"),
        name="in_proj",
    )(xcat, modtab, g, whg, wrw, wgt)


def _scan_tile(d, s, nt, nctx):
    bwd = jnp.where(s < nctx, nctx - 1 - s, nt - 1 - (s - nctx))
    return jnp.where(d == 0, s, bwd)


def _hgrn_stack():
    c = CHUNK
    out = np.zeros((2, (HG_MM_LEVELS + 1) * c, c), np.float32)
    for d in range(2):
        pi = np.arange(c) if d == 0 else c - 1 - np.arange(c)
        out[d, :c] = pi[None, :] <= pi[:, None]
        for lv in range(1, HG_MM_LEVELS + 1):
            m = 1 << lv
            mid = (pi // m) * m + m // 2
            late = (pi % m) >= m // 2
            sum_late = (pi[None, :] >= mid[:, None]) & (pi[None, :] <= pi[:, None])
            sum_early = (pi[None, :] > pi[:, None]) & (pi[None, :] <= mid[:, None] - 1)
            out[d, lv * c:(lv + 1) * c] = np.where(late[:, None], sum_late, sum_early)
    return out


def _hgrn_steps(q_ref, f_ref, i_ref, lb_ref, stack_ref, o_ref, st_ref, d):
    C = CHUNK
    H = HG_HEAD

    log_lb = lb_ref[d, 0:1, :]
    log1m_lb = lb_ref[d, 1:2, :]
    one_m_lb = lb_ref[d, 2:3, :]
    stack = stack_ref[d]

    row = lax.broadcasted_iota(jnp.int32, (C, HG_WIDTH), 0)
    pi = row if d == 0 else C - 1 - row
    ii = lax.broadcasted_iota(jnp.int32, (C, C), 0)
    jj = lax.broadcasted_iota(jnp.int32, (C, C), 1)
    nchunk = TILE // C
    hsl = [slice(h * H, (h + 1) * H) for h in range(HG_HEADS)]
    ntdot = lambda a, b: lax.dot_general(a, b, (((1,), (1,)), ((), ())), preferred_element_type=F32)

    def chunk_local(cj, out):
        r0 = cj * C
        xf = f_ref[0, pl.ds(r0, C), :]
        lsig = jnp.minimum(xf, 0.0) - jnp.log(1.0 + jnp.exp(-jnp.abs(xf)))
        t2 = log1m_lb + lsig
        mx = jnp.maximum(log_lb, t2)
        logf = mx + jnp.log(1.0 + jnp.exp(-jnp.abs(log_lb - t2)))
        kin = one_m_lb * _sigmoid(-xf)
        qr = q_ref[0, pl.ds(r0, C), :]
        q = qr * _sigmoid(qr) * (HG_HEAD ** -0.5)
        vb = i_ref[0, pl.ds(r0, C), :].astype(BF16)

        e_all = _dot_exact_lhs(stack, logf)
        cs = e_all[0:C]
        tot = jnp.sum(logf, axis=0, keepdims=True)
        qb = q.astype(BF16)
        kb = kin.astype(BF16)
        scores = [jnp.where(ii == jj, ntdot(qb[:, ls], kb[:, ls]), 0.0) for ls in hsl]
        yield
        for lv in range(1, LEVELS + 1):
            m = 1 << lv
            late = ((pi >> (lv - 1)) & 1) == 1
            if lv <= HG_MM_LEVELS:
                e = e_all[lv * C:(lv + 1) * C]
            else:
                at_row = [b * m + m // 2 - (1 if d == 0 else 0) for b in range(C // m)]
                cmid = jnp.concatenate(
                    [jnp.broadcast_to(cs[r:r + 1], (m, HG_WIDTH)) for r in at_row], axis=0)
                e = jnp.where(late, cs - cmid, cmid - cs)
            ex = jnp.exp(e)
            qm = jnp.where(late, q * ex, 0.0).astype(BF16)
            km = jnp.where(late, 0.0, kin * ex).astype(BF16)
            sc = [ntdot(qm[:, ls], km[:, ls]) for ls in hsl]
            if lv < LEVELS:
                same = (ii >> lv) == (jj >> lv)
                sc = [jnp.where(same, x, 0.0) for x in sc]
            scores = [a + b for a, b in zip(scores, sc)]
            yield
        q_in = (q * jnp.exp(cs)).astype(BF16)
        k_end = (kin * jnp.exp(tot - cs)).astype(BF16)
        o_loc = [jnp.dot(scores[h].astype(BF16), vb[:, ls], preferred_element_type=F32)
                 for h, ls in enumerate(hsl)]
        kv = [lax.dot_general(vb[:, ls], k_end[:, ls], (((0,), (0,)), ((), ())),
                              preferred_element_type=F32) for ls in hsl]
        out.append((r0, q_in, o_loc, kv, jnp.exp(tot)))
        yield

    loc = []
    for step in range(nchunk):
        yield from chunk_local(step if d == 0 else nchunk - 1 - step, loc)
    st = [st_ref[d, h] for h in range(HG_HEADS)]
    for r0, q_in, o_loc, kv, dec_end in loc:
        for h, ls in enumerate(hsl):
            o_ref[0, pl.ds(r0, C), ls] = (ntdot(q_in[:, ls], st[h].astype(BF16))
                                          + o_loc[h]).astype(o_ref.dtype)
        st = [st[h] * dec_end[:, ls] + kv[h] for h, ls in enumerate(hsl)]
        yield
    for h in range(HG_HEADS):
        st_ref[d, h] = st[h]


def _rw_tri():
    out = np.zeros((2, TILE, TILE), np.float32)
    idx = np.arange(TILE)
    same = (idx[:, None] // CHUNK) == (idx[None, :] // CHUNK)
    out[0] = same & (idx[None, :] <= idx[:, None])
    out[1] = same & (idx[None, :] >= idx[:, None])
    return out


def _head_ones():
    idx = np.arange(RW_WIDTH)
    return ((idx[:, None] // RW_HEAD) == (idx[None, :] // RW_HEAD)).astype(np.float32)


def _rwkv_pre_kernel(cur_ref, up_ref, dn_ref, mu_ref, vec_ref, dvec_ref, w2_ref, a2_ref, g2_ref,
                     hones_ref, tri_ref, at_ref, rt_ref, bt_ref, kt_ref, v_ref, gam_ref, bon_ref, g_ref,
                     *, nt, nctx):
    t = pl.program_id(1)
    C = CHUNK
    W = RW_WIDTH

    p = cur_ref[0]
    is_ctx = t < nctx
    rowi = lax.broadcasted_iota(jnp.int32, p.shape, 0)
    lane = lax.broadcasted_iota(jnp.int32, p.shape, 1)
    prev1 = pltpu.roll(p, 1, 0)
    next1 = pltpu.roll(p, TILE - 1, 0)
    seg = jnp.where(is_ctx, TILE - 1, GRID_W - 1)
    prev1 = jnp.where((rowi & seg) == 0, 0.0, prev1)
    next1 = jnp.where((rowi & seg) == seg, 0.0, next1)
    up_ok = t > nctx
    dn_ok = jnp.logical_and(t >= nctx, t < nt - 1)
    up = jnp.concatenate([jnp.where(up_ok, up_ref[0], 0.0), p[:TILE - GRID_W]], axis=0)
    dn = jnp.concatenate([p[GRID_W:], jnp.where(dn_ok, dn_ref[0], 0.0)], axis=0)
    cls = lane & 3
    sh_even = jnp.where(cls == 0, prev1, jnp.where(is_ctx, prev1, up))
    sh_odd = jnp.where(cls == 1, next1, jnp.where(is_ctx, next1, dn))
    shifted = jnp.where((cls & 1) == 0, sh_even, sh_odd)
    pf = p + mu_ref[...] * (shifted - p)

    r = pf[:, 0:W]
    k = pf[:, W:2 * W]
    v = pf[:, 2 * W:3 * W]
    o3 = 3 * W
    wd = pf[:, o3:o3 + 2 * RW_LORA]
    ad = pf[:, o3 + 2 * RW_LORA:o3 + 4 * RW_LORA]
    gd = pf[:, o3 + 4 * RW_LORA:o3 + 4 * RW_LORA + RW_GATE_LORA]

    k_k = vec_ref[0:1, :]
    k_a = vec_ref[1:2, :]
    r_k = vec_ref[2:3, :]
    hones = hones_ref[...]

    kkr = k * k_k
    ss = _dot_exact_rhs(kkr * kkr, hones)
    kk = kkr * lax.rsqrt(jnp.maximum(ss, 1e-24))
    g_ref[0] = _bdot(_sigmoid(gd), g2_ref[...]).astype(g_ref.dtype)
    v_ref[0] = v.astype(BF16)
    twd = jnp.tanh(wd).astype(BF16)
    adb = ad.astype(BF16)
    ksum = None
    for dd in range(2):
        lw = dvec_ref[dd, 0:1, :] + jnp.dot(twd, w2_ref[dd], preferred_element_type=F32)
        ld = -float(np.exp(-0.5)) * _sigmoid(lw)
        a = _sigmoid(dvec_ref[dd, 1:2, :] + jnp.dot(adb, a2_ref[dd], preferred_element_type=F32))
        kdir = k * (1.0 + (a - 1.0) * k_a)
        ksum = kdir if ksum is None else ksum + kdir
        cs = _dot_exact_lhs(tri_ref[dd], ld)
        einv = jnp.exp(-cs)
        at_ref[dd, 0] = (-kk * jnp.exp(cs - ld)).astype(BF16)
        rt_ref[dd, 0] = (r * jnp.exp(cs)).astype(BF16)
        bt_ref[dd, 0] = (kk * a * einv).astype(BF16)
        kt_ref[dd, 0] = (kdir * einv).astype(BF16)
        for c in range(TILE // C):
            tot = jnp.sum(ld[c * C:(c + 1) * C], axis=0, keepdims=True)
            gam_ref[dd, 0, c * 8:(c + 1) * 8, :] = jnp.broadcast_to(jnp.exp(tot), (8, W))
    bon_ref[0] = (_dot_exact_rhs(r * ksum * r_k, hones) * v).astype(bon_ref.dtype)


def _mixer_kernel(lb_ref, stack_ref, *refs):
    nin = 9
    per_dir = [refs[z * nin:(z + 1) * nin] for z in range(2)]
    o_refs = [refs[2 * nin], refs[2 * nin + 2]]
    y_refs = [refs[2 * nin + 1], refs[2 * nin + 3]]
    st_ref, s_ref = refs[2 * nin + 4:]
    s = pl.program_id(1)
    C = CHUNK
    nchunk = TILE // C
    dirs = (0, 1)

    @pl.when(s == 0)
    def _():
        st_ref[...] = jnp.zeros_like(st_ref)
        s_ref[...] = jnp.zeros_like(s_ref)

    hgrn = [_hgrn_steps(*per_dir[z][:3], lb_ref, stack_ref, o_refs[z], st_ref, z) for z in dirs]

    def fill(n):
        for _ in range(n):
            for gen in hgrn:
                next(gen, None)

    at_ref, rt_ref, bt_ref, kt_ref, v_ref, gam_ref = zip(*[per_dir[z][3:] for z in dirs])
    pi_ = lax.broadcasted_iota(jnp.int32, (PAIR, PAIR), 0)
    pj_ = lax.broadcasted_iota(jnp.int32, (PAIR, PAIR), 1)
    same_head = (pi_ < RW_HEAD) == (pj_ < RW_HEAD)
    eye_pair = (pi_ == pj_).astype(F32)
    head0 = lax.broadcasted_iota(jnp.int32, (C, PAIR), 1) < RW_HEAD

    ii4 = lax.broadcasted_iota(jnp.int32, (C, 2 * PAIR), 0)
    jj4 = lax.broadcasted_iota(jnp.int32, (C, 2 * PAIR), 1) & (C - 1)
    before4 = [jj4 < ii4, jj4 > ii4]
    upto4 = [jj4 <= ii4, jj4 >= ii4]
    eye2 = (lax.broadcasted_iota(jnp.int32, (C, PAIR), 0)
            == (lax.broadcasted_iota(jnp.int32, (C, PAIR), 1) & (C - 1))).astype(F32)
    npair = RW_HEADS // 2
    lsl = [slice(pr * PAIR, (pr + 1) * PAIR) for pr in range(npair)]
    pairs = [(z, u, pr) for z in dirs for u in range(nchunk) for pr in range(npair)]
    fdot = lambda a, b: jnp.dot(a, b, preferred_element_type=F32)
    cat0 = lambda xs: jnp.concatenate(xs, axis=0)

    def bd(x):
        x = x.astype(F32)
        return cat0([jnp.where(head0, x, 0.0), jnp.where(head0, 0.0, x)]).astype(BF16)

    def rows(z, u):
        cj = u if z == 0 else nchunk - 1 - u
        return slice(cj * C, (cj + 1) * C)

    def gam_row(z, u):
        cj = u if z == 0 else nchunk - 1 - u
        return slice(cj * 8, cj * 8 + 1)

    at = {(z, u, pr): at_ref[z][0, 0, rows(z, u), lsl[pr]] for z, u, pr in pairs}
    rt = {(z, u, pr): rt_ref[z][0, 0, rows(z, u), lsl[pr]] for z, u, pr in pairs}
    bt = {(z, u, pr): bt_ref[z][0, 0, rows(z, u), lsl[pr]] for z, u, pr in pairs}
    kt = {(z, u, pr): kt_ref[z][0, 0, rows(z, u), lsl[pr]] for z, u, pr in pairs}
    vv = {(z, u, pr): v_ref[z][0, rows(z, u), lsl[pr]] for z, u, pr in pairs}
    gam = {(z, u, pr): gam_ref[z][0, 0, gam_row(z, u), lsl[pr]] for z, u, pr in pairs}
    bk = {pq: cat0([bt[pq], kt[pq]]) for pq in pairs}
    bdv = {pq: bd(vv[pq]) for pq in pairs}
    top, bot = {}, {}
    for pq in pairs:
        lhs = cat0([at[pq], rt[pq]])
        x = _bdot_nt(lhs, cat0([bd(bt[pq]), bd(kt[pq])]))
        top[pq] = jnp.where(before4[pq[0]], x[:C], 0.0)
        bot[pq] = jnp.where(upto4[pq[0]], x[C:], 0.0).astype(BF16)
    fill(HG_FILL)
    nf = {pq: top[pq][:, :PAIR] for pq in pairs}
    tinv = {pq: eye2 + nf[pq] for pq in pairs}
    pw = {pq: fdot(nf[pq].astype(BF16), bd(nf[pq])) for pq in pairs}
    w = {pq: fdot(top[pq][:, PAIR:].astype(BF16), bdv[pq]) for pq in pairs}
    fill(HG_FILL)
    for _ in range(LEVELS - 2):
        res = {pq: fdot(cat0([tinv[pq], pw[pq]]).astype(BF16), bd(pw[pq])) for pq in pairs}
        tinv = {pq: tinv[pq] + res[pq][:C] for pq in pairs}
        pw = {pq: res[pq][C:] for pq in pairs}
        fill(HG_FILL)
    tinv = {pq: tinv[pq] + fdot(tinv[pq].astype(BF16), bd(pw[pq])) for pq in pairs}
    tb = {pq: tinv[pq].astype(BF16) for pq in pairs}
    fill(HG_FILL)
    ahf = {pq: fdot(tb[pq], bd(at[pq])) for pq in pairs}
    uvf = {pq: fdot(tb[pq], bd(w[pq])) for pq in pairs}
    fill(HG_FILL)
    rhat = {pq: rt[pq] + fdot(bot[pq][:, :PAIR], bd(ahf[pq])) for pq in pairs}
    yloc = {pq: fdot(bot[pq], cat0([bd(uvf[pq]), bdv[pq]])) for pq in pairs}
    mm = {pq: jnp.where(same_head, eye_pair + _bdot_tn(ahf[pq], bk[pq][:C]), 0.0) * gam[pq]
          for pq in pairs}
    nn = {pq: jnp.where(same_head, _bdot_tn(cat0([uvf[pq].astype(BF16), vv[pq]]), bk[pq]), 0.0)
          * gam[pq] for pq in pairs}
    fill(HG_FILL)
    st = {(z, pr): s_ref[z, pr] for z in dirs for pr in range(npair)}
    for u in range(nchunk):
        stb = {k: x.astype(BF16) for k, x in st.items()}
        for z in dirs:
            for pr in range(npair):
                y_refs[z][0, rows(z, u), lsl[pr]] = (
                    _bdot_nt(rhat[z, u, pr], stb[z, pr]) + yloc[z, u, pr]).astype(y_refs[z].dtype)
        st = {(z, pr): _bdot(stb[z, pr], mm[z, u, pr]) + nn[z, u, pr] for z, pr in st}
        fill(HG_FILL)
    for (z, pr), x in st.items():
        s_ref[z, pr] = x
    for gen in hgrn:
        for _ in gen:
            pass


def _rwkv_pre(p_rw, mu, vecs, dvecs, w2cat, a2cat, g2, hones, tri, nctx):
    B, TC, _ = p_rw.shape
    nt = TC // TILE
    hb = TILE // GRID_W
    nhb = TC // GRID_W
    grows = 8 * (TILE // CHUNK)
    full = lambda a: pl.BlockSpec(a.shape, lambda b, t: (0,) * a.ndim)
    both = pl.BlockSpec((2, 1, TILE, RW_WIDTH), lambda b, t: (0, b, t, 0))
    one = pl.BlockSpec((1, TILE, RW_WIDTH), lambda b, t: (b, t, 0))
    dir_bf = jax.ShapeDtypeStruct((2, B, TC, RW_WIDTH), BF16)
    return pl.pallas_call(
        functools.partial(_rwkv_pre_kernel, nt=nt, nctx=nctx),
        grid=(B, nt),
        in_specs=[pl.BlockSpec((1, TILE, RW_COLS), lambda b, t: (b, t, 0)),
                  pl.BlockSpec((1, GRID_W, RW_COLS), lambda b, t: (b, jnp.maximum(t * hb - 1, 0), 0)),
                  pl.BlockSpec((1, GRID_W, RW_COLS),
                               lambda b, t: (b, jnp.minimum((t + 1) * hb, nhb - 1), 0)),
                  full(mu), full(vecs), full(dvecs), full(w2cat), full(a2cat), full(g2),
                  full(hones), full(tri)],
        out_specs=[both, both, both, both, one,
                   pl.BlockSpec((2, 1, grows, RW_WIDTH), lambda b, t: (0, b, t, 0)), one, one],
        out_shape=[dir_bf, dir_bf, dir_bf, dir_bf, jax.ShapeDtypeStruct((B, TC, RW_WIDTH), BF16),
                   jax.ShapeDtypeStruct((2, B, nt * grows, RW_WIDTH), F32),
                   jax.ShapeDtypeStruct((B, TC, RW_WIDTH), BF16),
                   jax.ShapeDtypeStruct((B, TC, RW_WIDTH), BF16)],
        compiler_params=_params("arbitrary", "arbitrary"),
        name="rwkv7_pre",
    )(p_rw, p_rw, p_rw, mu, vecs, dvecs, w2cat, a2cat, g2, hones, tri)


def _mixer(p_hg, lbtab, stack, at, rt, bt, kt, v, gam, nctx):
    _, B, TC, _ = at.shape
    nt = TC // TILE
    grows = 8 * (TILE // CHUNK)
    assert HG_WIDTH == RW_WIDTH
    full = lambda a: pl.BlockSpec(a.shape, lambda b, s: (0,) * a.ndim)
    in_specs, operands = [full(lbtab), full(stack)], [lbtab, stack]
    out_specs = []
    for z in range(2):
        tile = functools.partial(_scan_tile, z, nt=nt, nctx=nctx)
        col = lambda c, tile=tile: pl.BlockSpec((1, TILE, HG_WIDTH), lambda b, s: (b, tile(s), c))
        dir_tile = pl.BlockSpec((1, 1, TILE, RW_WIDTH), lambda b, s, z=z, tile=tile: (z, b, tile(s), 0))
        in_specs += [col(0), col(1 + z), col(3), dir_tile, dir_tile, dir_tile, dir_tile, col(0),
                     pl.BlockSpec((1, 1, grows, RW_WIDTH),
                                  lambda b, s, z=z, tile=tile: (z, b, tile(s), 0))]
        operands += [p_hg, p_hg, p_hg, at, rt, bt, kt, v, gam]
        out_specs += [col(0), col(0)]
    out_shape = jax.ShapeDtypeStruct((B, TC, RW_WIDTH), BF16)
    return pl.pallas_call(
        _mixer_kernel,
        grid=(B, nt),
        in_specs=in_specs,
        out_specs=out_specs,
        out_shape=[out_shape] * 4,
        scratch_shapes=[pltpu.VMEM((2, HG_HEADS, HG_HEAD, HG_HEAD), F32),
                        pltpu.VMEM((2, RW_HEADS // 2, PAIR, PAIR), F32)],
        compiler_params=_params("arbitrary", "arbitrary"),
        name="mixer_scans",
    )(*operands)


def _merge_kernel(of_ref, ob_ref, og_ref, yf_ref, yb_ref, bon_ref, g_ref, gt_ref, x_ref, mod_ref,
                  hgn_ref, gn_ref, ng_ref, hones_ref, pa_ref, pb_ref, wo_ref, o_ref):
    D = x_ref.shape[-1]
    H = HG_HEAD
    oh = of_ref[0].astype(F32) + ob_ref[0].astype(F32)
    og = og_ref[0]
    hgn = hgn_ref[...]
    parts = []
    for h in range(HG_HEADS):
        ls = slice(h * H, (h + 1) * H)
        parts.append(_rms(oh[:, ls], hgn) * (og[:, ls] * _sigmoid(og[:, ls])))
    o_hg = jnp.concatenate(parts, axis=-1)
    hones = hones_ref[...]
    y = yf_ref[0].astype(F32) + yb_ref[0].astype(F32)
    mean = _dot_exact_rhs(y, hones) * (1.0 / RW_HEAD)
    yc = y - mean
    var = _dot_exact_rhs(yc * yc, hones) * (1.0 / RW_HEAD)
    yn = yc * lax.rsqrt(var + RW_GN_EPS) * gn_ref[0:1, :] + gn_ref[1:2, :]
    o_rw = (yn + bon_ref[0].astype(F32)) * g_ref[0].astype(F32)
    gt = gt_ref[0].astype(F32)
    m = (_sigmoid(gt[:, :D]) * _bdot(o_hg, pa_ref[...])
         + _sigmoid(gt[:, D:]) * _bdot(o_rw, pb_ref[...]))
    yx = _bdot(m, wo_ref[...])
    gate = mod_ref[0, :, 2 * D:3 * D]
    o_ref[0] = x_ref[0] + gate * _rms(yx, ng_ref[...])


def _merge(o_f, o_b, p_hg, y_f, y_b, bon, g, p_gt, xcat, modtab, hgn, gn, ng, hones, pa, pb, wo, t0):
    B, TC, D = xcat.shape
    nt = TC // TILE - t0
    full = lambda a: pl.BlockSpec(a.shape, lambda b, t: (0,) * a.ndim)
    tok = pl.BlockSpec((1, TILE, RW_WIDTH), lambda b, t: (b, t + t0, 0))
    return pl.pallas_call(
        _merge_kernel,
        grid=(B, nt),
        in_specs=[tok, tok,
                  pl.BlockSpec((1, TILE, HG_WIDTH), lambda b, t: (b, t + t0, 4)),
                  tok, tok, tok, tok,
                  pl.BlockSpec((1, TILE, 2 * D), lambda b, t: (b, t + t0, 0)),
                  pl.BlockSpec((1, TILE, D), lambda b, t: (b, t + t0, 0)),
                  pl.BlockSpec((1, 1, modtab.shape[-1]),
                               lambda b, t: (2 * b + jnp.minimum(t + t0, 1), 0, 0)),
                  full(hgn), full(gn), full(ng), full(hones), full(pa), full(pb), full(wo)],
        out_specs=pl.BlockSpec((1, TILE, D), lambda b, t: (b, t, 0)),
        out_shape=jax.ShapeDtypeStruct((B, nt * TILE, D), F32),
        compiler_params=_params("arbitrary", "arbitrary"),
        name="gated_merge",
    )(o_f, o_b, p_hg, y_f, y_b, bon, g, p_gt, xcat, modtab, hgn, gn, ng, hones, pa, pb, wo)


def _ffn_kernel(x_ref, mod_ref, g_ref, w1_ref, w2_ref, o_ref):
    D = x_ref.shape[-1]
    F = w2_ref.shape[0]
    x = x_ref[0]
    h = _rms(x, g_ref[0:1, :]) * (1.0 + mod_ref[0, :, 4 * D:5 * D]) + mod_ref[0, :, 3 * D:4 * D]
    hb = h.astype(BF16)
    gt = jnp.dot(hb, w1_ref[:, 0:F], preferred_element_type=F32)
    up = jnp.dot(hb, w1_ref[:, F:2 * F], preferred_element_type=F32)
    act = (gt * _sigmoid(gt) * up).astype(BF16)
    y = jnp.dot(act, w2_ref[...], preferred_element_type=F32)
    o_ref[0] = x + mod_ref[0, :, 5 * D:6 * D] * _rms(y, g_ref[1:2, :])


def _ffn(x, modtab, g, w1, w2, t0):
    B, TX, D = x.shape
    nt = TX // TILE
    full = lambda a: pl.BlockSpec(a.shape, lambda b, t: (0,) * a.ndim)
    return pl.pallas_call(
        _ffn_kernel,
        grid=(B, nt),
        in_specs=[pl.BlockSpec((1, TILE, D), lambda b, t: (b, t, 0)),
                  pl.BlockSpec((1, 1, modtab.shape[-1]),
                               lambda b, t: (2 * b + jnp.minimum(t + t0, 1), 0, 0)),
                  full(g), full(w1), full(w2)],
        out_specs=pl.BlockSpec((1, TILE, D), lambda b, t: (b, t, 0)),
        out_shape=jax.ShapeDtypeStruct((B, TX, D), F32),
        compiler_params=_params("arbitrary", "arbitrary"),
        name="swiglu_ffn",
    )(x, modtab, g, w1, w2)


def kernel(x, c, ctx, c_ctx, ada_w, ada_b, norm_g, w_in, hg_lb_logits, hg_norm_g, rw_mu, rw_w0, rw_w2,
           rw_a0, rw_a2, rw_g2, rw_kk, rw_ka, rw_rk, rw_gn_g, rw_gn_b, proj_a, proj_b, w_out, ffn_w1,
           ffn_w2):
    B, T, D = x.shape
    LCTX = ctx.shape[1]
    L = ada_w.shape[0]
    assert T % TILE == 0 and LCTX == TILE and TILE % GRID_W == 0 and T // TILE >= 2
    nctx = LCTX // TILE

    rows = -(-(B + 1) // 8) * 8
    cond = jnp.zeros((rows, D), F32).at[:B].set(c).at[B].set(c_ctx)
    mods = _mods(cond, ada_w, ada_b)

    lb_cum = jnp.cumsum(jax.nn.softmax(hg_lb_logits.astype(F32), axis=0), axis=0)
    hg_lb = lb_cum - lb_cum[:1]
    lbtab = jnp.stack([jnp.log(hg_lb), jnp.log1p(-hg_lb), 1.0 - hg_lb], axis=2)

    stack = jnp.asarray(_hgrn_stack(), BF16)
    tri = jnp.asarray(_rw_tri(), BF16)
    hones = jnp.asarray(_head_ones(), BF16)

    zeros = jnp.zeros_like(rw_w2[:, 0])
    w2cat = jnp.stack([jnp.concatenate([rw_w2[:, 0], zeros], axis=1),
                       jnp.concatenate([zeros, rw_w2[:, 1]], axis=1)], axis=1).astype(BF16)
    a2cat = jnp.stack([jnp.concatenate([rw_a2[:, 0], zeros], axis=1),
                       jnp.concatenate([zeros, rw_a2[:, 1]], axis=1)], axis=1).astype(BF16)

    xcat = jnp.concatenate([ctx, x], axis=1)
    for l in range(L):
        last = l == L - 1
        lat = mods[l, :B]
        ctxm = jnp.broadcast_to(mods[l, B], lat.shape)
        modtab = jnp.stack([ctxm, lat], axis=1).reshape(2 * B, 1, 6 * D)
        wl = w_in[l].astype(BF16)
        p_hg, p_rw, p_gt = _inproj(xcat, modtab, norm_g[l, 0:1], wl[:, :HG_COLS],
                                   wl[:, HG_COLS:HG_COLS + RW_COLS], wl[:, HG_COLS + RW_COLS:])
        vecs = jnp.stack([rw_kk[l], rw_ka[l], rw_rk[l].reshape(-1)], axis=0)
        dvecs = jnp.stack([rw_w0[l], rw_a0[l]], axis=1)
        at, rt, bt, kt, v, gam, bon, g = _rwkv_pre(p_rw, rw_mu[l][None, :], vecs, dvecs, w2cat[l],
                                                   a2cat[l], rw_g2[l].astype(BF16), hones, tri, nctx)
        o_f, y_f, o_b, y_b = _mixer(p_hg, lbtab[l], stack, at, rt, bt, kt, v, gam, nctx)
        t0 = nctx if last else 0
        xm = _merge(o_f, o_b, p_hg, y_f, y_b, bon, g, p_gt, xcat, modtab, hg_norm_g[l][None, :],
                    jnp.stack([rw_gn_g[l], rw_gn_b[l]], axis=0), norm_g[l, 1:2], hones,
                    proj_a[l].astype(BF16), proj_b[l].astype(BF16), w_out[l].astype(BF16), t0)
        xcat = _ffn(xm, modtab, norm_g[l, 2:4], ffn_w1[l].astype(BF16), ffn_w2[l].astype(BF16), t0)
    return xcat
```

```python
import functools

import numpy as np
import jax
import jax.numpy as jnp
from jax import lax
from jax.experimental import pallas as pl
from jax.experimental.pallas import tpu as pltpu

F32 = jnp.float32
BF16 = jnp.bfloat16

NORM_EPS = 1e-6
GRID_W = 64
HG_HEADS = 4
HG_HEAD = 128
HG_WIDTH = HG_HEADS * HG_HEAD
RW_HEADS = 8
RW_HEAD = 64
RW_WIDTH = RW_HEADS * RW_HEAD
RW_LORA = 64
RW_GATE_LORA = 128
RW_GN_EPS = 64e-5
RW_COLS = 3 * RW_WIDTH + 4 * RW_LORA + RW_GATE_LORA
HG_COLS = 5 * HG_WIDTH

TILE = 256
CHUNK = 64
LEVELS = 6
HG_MM_LEVELS = 3
HG_FILL = 9
PAIR = 2 * RW_HEAD
VMEM_LIMIT = 56 * 1024 * 1024


def _bdot(a, b):
    return jnp.dot(a.astype(BF16), b.astype(BF16), preferred_element_type=F32)


def _bdot_nt(a, b):
    return lax.dot_general(a.astype(BF16), b.astype(BF16), (((1,), (1,)), ((), ())),
                           preferred_element_type=F32)


def _bdot_tn(a, b):
    return lax.dot_general(a.astype(BF16), b.astype(BF16), (((0,), (0,)), ((), ())),
                           preferred_element_type=F32)


def _split3(x):
    hi = x.astype(BF16)
    r1 = x - hi.astype(F32)
    mid = r1.astype(BF16)
    lo = (r1 - mid.astype(F32)).astype(BF16)
    return hi, mid, lo


def _dot_exact_lhs(m01, x, terms=2):
    d = lambda t: jnp.dot(m01, t, preferred_element_type=F32)
    return sum(d(t) for t in _split3(x)[:terms])


def _dot_exact_rhs(x, m01, terms=2):
    d = lambda t: jnp.dot(t, m01, preferred_element_type=F32)
    return sum(d(t) for t in _split3(x)[:terms])


def _rms(x, g):
    ms = jnp.mean(x * x, axis=-1, keepdims=True)
    return x * lax.rsqrt(ms + NORM_EPS) * g


def _sigmoid(x):
    return 0.5 * jnp.tanh(0.5 * x) + 0.5


def _params(*sem):
    return pltpu.CompilerParams(dimension_semantics=sem, vmem_limit_bytes=VMEM_LIMIT)


def _mods_kernel(c_ref, w_ref, b_ref, o_ref):
    c = c_ref[...]
    s = c * _sigmoid(c)
    sh, sm, sl = _split3(s)
    wh, wm, wl = _split3(w_ref[0])
    d = lambda a, b: jnp.dot(a, b, preferred_element_type=F32)
    acc = d(sh, wh) + d(sh, wm) + d(sm, wh) + d(sh, wl) + d(sl, wh) + d(sm, wm)
    o_ref[0] = acc + b_ref[0]


def _mods(cond, ada_w, ada_b):
    L, D, N = ada_w.shape
    rows = cond.shape[0]
    tn = 1536
    return pl.pallas_call(
        _mods_kernel,
        grid=(L, N // tn),
        in_specs=[pl.BlockSpec((rows, D), lambda l, n: (0, 0)),
                  pl.BlockSpec((1, D, tn), lambda l, n: (l, 0, n)),
                  pl.BlockSpec((1, 1, tn), lambda l, n: (l, 0, n))],
        out_specs=pl.BlockSpec((1, rows, tn), lambda l, n: (l, 0, n)),
        out_shape=jax.ShapeDtypeStruct((L, rows, N), F32),
        compiler_params=_params("arbitrary", "arbitrary"),
        name="adaln_mods",
    )(cond, ada_w, ada_b.reshape(L, 1, N))


def _inproj_kernel(x_ref, mod_ref, g_ref, whg_ref, wrw_ref, wgt_ref, ohg_ref, orw_ref, ogt_ref):
    D = x_ref.shape[-1]
    y = _rms(x_ref[0], g_ref[...])
    shift = mod_ref[0, :, 0:D]
    scale = mod_ref[0, :, D:2 * D]
    h = (y * (1.0 + scale) + shift).astype(BF16)
    ohg_ref[0] = jnp.dot(h, whg_ref[...], preferred_element_type=F32)
    orw_ref[0] = jnp.dot(h, wrw_ref[...], preferred_element_type=F32)
    ogt_ref[0] = jnp.dot(h, wgt_ref[...], preferred_element_type=F32).astype(ogt_ref.dtype)


def _inproj(xcat, modtab, g, whg, wrw, wgt):
    B, TC, D = xcat.shape
    nt = TC // TILE
    full = lambda a: pl.BlockSpec(a.shape, lambda b, t: (0,) * a.ndim)
    outs = [jax.ShapeDtypeStruct((B, TC, w.shape[1]), dt) for w, dt in ((whg, F32), (wrw, F32), (wgt, BF16))]
    return pl.pallas_call(
        _inproj_kernel,
        grid=(B, nt),
        in_specs=[pl.BlockSpec((1, TILE, D), lambda b, t: (b, t, 0)),
                  pl.BlockSpec((1, 1, modtab.shape[-1]), lambda b, t: (2 * b + jnp.minimum(t, 1), 0, 0)),
                  full(g), full(whg), full(wrw), full(wgt)],
        out_specs=[pl.BlockSpec((1, TILE, o.shape[-1]), lambda b, t: (b, t, 0)) for o in outs],
        out_shape=outs,
        compiler_params=_params("arbitrary", "arbitrary"),
        name="in_proj",
    )(xcat, modtab, g, whg, wrw, wgt)


def _scan_tile(d, s, nt, nctx):
    bwd = jnp.where(s < nctx, nctx - 1 - s, nt - 1 - (s - nctx))
    return jnp.where(d == 0, s, bwd)


def _hgrn_stack():
    c = CHUNK
    out = np.zeros((2, (HG_MM_LEVELS + 1) * c, c), np.float32)
    for d in range(2):
        pi = np.arange(c) if d == 0 else c - 1 - np.arange(c)
        out[d, :c] = pi[None, :] <= pi[:, None]
        for lv in range(1, HG_MM_LEVELS + 1):
            m = 1 << lv
            mid = (pi // m) * m + m // 2
            late = (pi % m) >= m // 2
            sum_late = (pi[None, :] >= mid[:, None]) & (pi[None, :] <= pi[:, None])
            sum_early = (pi[None, :] > pi[:, None]) & (pi[None, :] <= mid[:, None] - 1)
            out[d, lv * c:(lv + 1) * c] = np.where(late[:, None], sum_late, sum_early)
    return out


def _hgrn_steps(q_ref, f_ref, i_ref, lb_ref, stack_ref, o_ref, st_ref, d):
    C = CHUNK
    H = HG_HEAD

    log_lb = lb_ref[d, 0:1, :]
    log1m_lb = lb_ref[d, 1:2, :]
    one_m_lb = lb_ref[d, 2:3, :]
    stack = stack_ref[d]

    row = lax.broadcasted_iota(jnp.int32, (C, HG_WIDTH), 0)
    pi = row if d == 0 else C - 1 - row
    ii = lax.broadcasted_iota(jnp.int32, (C, C), 0)
    jj = lax.broadcasted_iota(jnp.int32, (C, C), 1)
    valid = {}
    for lv in range(1, LEVELS + 1):
        hi = (ii if d == 0 else C - 1 - ii) >> (lv - 1)
        hj = (jj if d == 0 else C - 1 - jj) >> (lv - 1)
        valid[lv] = (hi - hj + ((hj & 1) << 8)) == 1
    nchunk = TILE // C
    hsl = [slice(h * H, (h + 1) * H) for h in range(HG_HEADS)]
    ntdot = lambda a, b: lax.dot_general(a, b, (((1,), (1,)), ((), ())), preferred_element_type=F32)

    def chunk_local(cj, out):
        r0 = cj * C
        xf = f_ref[0, pl.ds(r0, C), :]
        lsig = jnp.minimum(xf, 0.0) - jnp.log(1.0 + jnp.exp(-jnp.abs(xf)))
        t2 = log1m_lb + lsig
        mx = jnp.maximum(log_lb, t2)
        logf = mx + jnp.log(1.0 + jnp.exp(-jnp.abs(log_lb - t2)))
        kin = one_m_lb * _sigmoid(-xf)
        qr = q_ref[0, pl.ds(r0, C), :]
        q = qr * _sigmoid(qr) * (HG_HEAD ** -0.5)
        vb = i_ref[0, pl.ds(r0, C), :].astype(BF16)

        e_all = _dot_exact_lhs(stack, logf)
        cs = e_all[0:C]
        tot = jnp.sum(logf, axis=0, keepdims=True)
        qb = q.astype(BF16)
        kb = kin.astype(BF16)
        scores = [jnp.where(ii == jj, ntdot(qb[:, ls], kb[:, ls]), 0.0) for ls in hsl]
        yield
        for lv in range(1, LEVELS + 1):
            m = 1 << lv
            late = ((pi >> (lv - 1)) & 1) == 1
            if lv <= HG_MM_LEVELS:
                e = e_all[lv * C:(lv + 1) * C]
            else:
                at_row = [b * m + m // 2 - (1 if d == 0 else 0) for b in range(C // m)]
                cmid = jnp.concatenate(
                    [jnp.broadcast_to(cs[r:r + 1], (m, HG_WIDTH)) for r in at_row], axis=0)
                e = jnp.where(late, cs - cmid, cmid - cs)
            qk = (jnp.where(late, q, kin) * jnp.exp(e)).astype(BF16)
            scores = [jnp.where(valid[lv], ntdot(qk[:, ls], qk[:, ls]), sc)
                      for ls, sc in zip(hsl, scores)]
            yield
        q_in = (q * jnp.exp(cs)).astype(BF16)
        k_end = (kin * jnp.exp(tot - cs)).astype(BF16)
        o_loc = [jnp.dot(scores[h].astype(BF16), vb[:, ls], preferred_element_type=F32)
                 for h, ls in enumerate(hsl)]
        kv = [lax.dot_general(vb[:, ls], k_end[:, ls], (((0,), (0,)), ((), ())),
                              preferred_element_type=F32) for ls in hsl]
        out.append((r0, q_in, o_loc, kv, jnp.exp(tot)))
        yield

    loc = []
    for step in range(nchunk):
        yield from chunk_local(step if d == 0 else nchunk - 1 - step, loc)
    st = [st_ref[d, h] for h in range(HG_HEADS)]
    for r0, q_in, o_loc, kv, dec_end in loc:
        for h, ls in enumerate(hsl):
            o_ref[0, pl.ds(r0, C), ls] = (ntdot(q_in[:, ls], st[h].astype(BF16))
                                          + o_loc[h]).astype(o_ref.dtype)
        st = [st[h] * dec_end[:, ls] + kv[h] for h, ls in enumerate(hsl)]
        yield
    for h in range(HG_HEADS):
        st_ref[d, h] = st[h]


def _rw_tri():
    out = np.zeros((2, TILE, TILE), np.float32)
    idx = np.arange(TILE)
    same = (idx[:, None] // CHUNK) == (idx[None, :] // CHUNK)
    out[0] = same & (idx[None, :] <= idx[:, None])
    out[1] = same & (idx[None, :] >= idx[:, None])
    return out


def _head_ones():
    idx = np.arange(RW_WIDTH)
    return ((idx[:, None] // RW_HEAD) == (idx[None, :] // RW_HEAD)).astype(np.float32)


def _rwkv_pre_kernel(cur_ref, up_ref, dn_ref, mu_ref, vec_ref, dvec_ref, w2_ref, a2_ref, g2_ref,
                     hones_ref, tri_ref, rhat_ref, yloc_ref, mm_ref, nn_ref, bon_ref, g_ref,
                     *, nt, nctx):
    t = pl.program_id(1)
    C = CHUNK
    W = RW_WIDTH

    p = cur_ref[0]
    is_ctx = t < nctx
    rowi = lax.broadcasted_iota(jnp.int32, p.shape, 0)
    lane = lax.broadcasted_iota(jnp.int32, p.shape, 1)
    prev1 = pltpu.roll(p, 1, 0)
    next1 = pltpu.roll(p, TILE - 1, 0)
    seg = jnp.where(is_ctx, TILE - 1, GRID_W - 1)
    prev1 = jnp.where((rowi & seg) == 0, 0.0, prev1)
    next1 = jnp.where((rowi & seg) == seg, 0.0, next1)
    up_ok = t > nctx
    dn_ok = jnp.logical_and(t >= nctx, t < nt - 1)
    up = jnp.concatenate([jnp.where(up_ok, up_ref[0], 0.0), p[:TILE - GRID_W]], axis=0)
    dn = jnp.concatenate([p[GRID_W:], jnp.where(dn_ok, dn_ref[0], 0.0)], axis=0)
    cls = lane & 3
    sh_even = jnp.where(cls == 0, prev1, jnp.where(is_ctx, prev1, up))
    sh_odd = jnp.where(cls == 1, next1, jnp.where(is_ctx, next1, dn))
    shifted = jnp.where((cls & 1) == 0, sh_even, sh_odd)
    pf = p + mu_ref[...] * (shifted - p)

    r = pf[:, 0:W]
    k = pf[:, W:2 * W]
    v = pf[:, 2 * W:3 * W]
    o3 = 3 * W
    wd = pf[:, o3:o3 + 2 * RW_LORA]
    ad = pf[:, o3 + 2 * RW_LORA:o3 + 4 * RW_LORA]
    gd = pf[:, o3 + 4 * RW_LORA:o3 + 4 * RW_LORA + RW_GATE_LORA]

    k_k = vec_ref[0:1, :]
    k_a = vec_ref[1:2, :]
    r_k = vec_ref[2:3, :]
    hones = hones_ref[...]

    kkr = k * k_k
    ss = _dot_exact_rhs(kkr * kkr, hones)
    kk = kkr * lax.rsqrt(jnp.maximum(ss, 1e-24))
    g_ref[0] = _bdot(_sigmoid(gd), g2_ref[...]).astype(g_ref.dtype)
    vb = v.astype(BF16)
    twd = jnp.tanh(wd).astype(BF16)
    adb = ad.astype(BF16)
    ksum = None
    dirs = (0, 1)
    nchunk = TILE // C
    atd, rtd, btd, ktd, gamd = {}, {}, {}, {}, {}
    for dd in dirs:
        lw = dvec_ref[dd, 0:1, :] + jnp.dot(twd, w2_ref[dd], preferred_element_type=F32)
        ld = -float(np.exp(-0.5)) * _sigmoid(lw)
        a = _sigmoid(dvec_ref[dd, 1:2, :] + jnp.dot(adb, a2_ref[dd], preferred_element_type=F32))
        kdir = k * (1.0 + (a - 1.0) * k_a)
        ksum = kdir if ksum is None else ksum + kdir
        cs = _dot_exact_lhs(tri_ref[dd], ld)
        einv = jnp.exp(-cs)
        atd[dd] = (-kk * jnp.exp(cs - ld)).astype(BF16)
        rtd[dd] = (r * jnp.exp(cs)).astype(BF16)
        btd[dd] = (kk * a * einv).astype(BF16)
        ktd[dd] = (kdir * einv).astype(BF16)
        gamd[dd] = [jnp.exp(jnp.sum(ld[c * C:(c + 1) * C], axis=0, keepdims=True))
                    for c in range(nchunk)]
    bon_ref[0] = (_dot_exact_rhs(r * ksum * r_k, hones) * v).astype(bon_ref.dtype)

    pi_ = lax.broadcasted_iota(jnp.int32, (PAIR, PAIR), 0)
    pj_ = lax.broadcasted_iota(jnp.int32, (PAIR, PAIR), 1)
    same_head = (pi_ < RW_HEAD) == (pj_ < RW_HEAD)
    eye_pair = (pi_ == pj_).astype(F32)
    head0 = lax.broadcasted_iota(jnp.int32, (C, PAIR), 1) < RW_HEAD

    ii4 = lax.broadcasted_iota(jnp.int32, (C, 2 * PAIR), 0)
    jj4 = lax.broadcasted_iota(jnp.int32, (C, 2 * PAIR), 1) & (C - 1)
    before4 = [jj4 < ii4, jj4 > ii4]
    upto4 = [jj4 <= ii4, jj4 >= ii4]
    eye2 = (lax.broadcasted_iota(jnp.int32, (C, PAIR), 0)
            == (lax.broadcasted_iota(jnp.int32, (C, PAIR), 1) & (C - 1))).astype(F32)
    npair = RW_HEADS // 2
    lsl = [slice(pr * PAIR, (pr + 1) * PAIR) for pr in range(npair)]
    pairs = [(z, c, pr) for z in dirs for c in range(nchunk) for pr in range(npair)]
    fdot = lambda a, b: jnp.dot(a, b, preferred_element_type=F32)
    cat0 = lambda xs: jnp.concatenate(xs, axis=0)

    def bd(x):
        x = x.astype(F32)
        return cat0([jnp.where(head0, x, 0.0), jnp.where(head0, 0.0, x)]).astype(BF16)

    rows = lambda c: slice(c * C, (c + 1) * C)
    at = {(z, c, pr): atd[z][rows(c), lsl[pr]] for z, c, pr in pairs}
    rt = {(z, c, pr): rtd[z][rows(c), lsl[pr]] for z, c, pr in pairs}
    bt = {(z, c, pr): btd[z][rows(c), lsl[pr]] for z, c, pr in pairs}
    kt = {(z, c, pr): ktd[z][rows(c), lsl[pr]] for z, c, pr in pairs}
    vv = {(z, c, pr): vb[rows(c), lsl[pr]] for z, c, pr in pairs}
    gam = {(z, c, pr): gamd[z][c][:, lsl[pr]] for z, c, pr in pairs}
    bk = {pq: cat0([bt[pq], kt[pq]]) for pq in pairs}
    bdv = {pq: bd(vv[pq]) for pq in pairs}
    top, bot = {}, {}
    for pq in pairs:
        lhs = cat0([at[pq], rt[pq]])
        x = _bdot_nt(lhs, cat0([bd(bt[pq]), bd(kt[pq])]))
        top[pq] = jnp.where(before4[pq[0]], x[:C], 0.0)
        bot[pq] = jnp.where(upto4[pq[0]], x[C:], 0.0).astype(BF16)
    nf = {pq: top[pq][:, :PAIR] for pq in pairs}
    tinv = {pq: eye2 + nf[pq] for pq in pairs}
    pw = {pq: fdot(nf[pq].astype(BF16), bd(nf[pq])) for pq in pairs}
    w = {pq: fdot(top[pq][:, PAIR:].astype(BF16), bdv[pq]) for pq in pairs}
    for _ in range(LEVELS - 2):
        res = {pq: fdot(cat0([tinv[pq], pw[pq]]).astype(BF16), bd(pw[pq])) for pq in pairs}
        tinv = {pq: tinv[pq] + res[pq][:C] for pq in pairs}
        pw = {pq: res[pq][C:] for pq in pairs}
    tinv = {pq: tinv[pq] + fdot(tinv[pq].astype(BF16), bd(pw[pq])) for pq in pairs}
    tb = {pq: tinv[pq].astype(BF16) for pq in pairs}
    ahf = {pq: fdot(tb[pq], bd(at[pq])) for pq in pairs}
    uvf = {pq: fdot(tb[pq], bd(w[pq])) for pq in pairs}
    for z, c, pr in pairs:
        pq = (z, c, pr)
        rhat_ref[z, 0, rows(c), lsl[pr]] = (
            rt[pq] + fdot(bot[pq][:, :PAIR], bd(ahf[pq]))).astype(rhat_ref.dtype)
        yloc_ref[z, 0, rows(c), lsl[pr]] = fdot(
            bot[pq], cat0([bd(uvf[pq]), bdv[pq]])).astype(yloc_ref.dtype)
    for z, c, pr in pairs:
        pq = (z, c, pr)
        mm_ref[z, 0, c * npair + pr] = (
            jnp.where(same_head, eye_pair + _bdot_tn(ahf[pq], bk[pq][:C]), 0.0) * gam[pq]
        ).astype(mm_ref.dtype)
        nn_ref[z, 0, c * npair + pr] = jnp.where(
            same_head, _bdot_tn(cat0([uvf[pq].astype(BF16), vv[pq]]), bk[pq]), 0.0) * gam[pq]


def _mixer_kernel(lb_ref, stack_ref, *refs):
    nin = 7
    per_dir = [refs[z * nin:(z + 1) * nin] for z in range(2)]
    o_refs = [refs[2 * nin], refs[2 * nin + 2]]
    y_refs = [refs[2 * nin + 1], refs[2 * nin + 3]]
    st_ref, s_ref = refs[2 * nin + 4:]
    s = pl.program_id(1)
    C = CHUNK
    nchunk = TILE // C
    npair = RW_HEADS // 2
    dirs = (0, 1)
    lsl = [slice(pr * PAIR, (pr + 1) * PAIR) for pr in range(npair)]

    @pl.when(s == 0)
    def _():
        st_ref[...] = jnp.zeros_like(st_ref)
        s_ref[...] = jnp.zeros_like(s_ref)

    hgrn = [_hgrn_steps(*per_dir[z][:3], lb_ref, stack_ref, o_refs[z], st_ref, z) for z in dirs]

    def fill(n):
        for _ in range(n):
            for gen in hgrn:
                next(gen, None)

    rhat_ref, yloc_ref, mm_ref, nn_ref = zip(*[per_dir[z][3:] for z in dirs])
    st = {(z, pr): s_ref[z, pr] for z in dirs for pr in range(npair)}
    for u in range(nchunk):
        fill(HG_FILL)
        stb = {k: x.astype(BF16) for k, x in st.items()}
        for z in dirs:
            c = u if z == 0 else nchunk - 1 - u
            for pr in range(npair):
                y_refs[z][0, c * C:(c + 1) * C, lsl[pr]] = (
                    _bdot_nt(rhat_ref[z][0, 0, c * C:(c + 1) * C, lsl[pr]], stb[z, pr])
                    + yloc_ref[z][0, 0, c * C:(c + 1) * C, lsl[pr]].astype(F32)
                ).astype(y_refs[z].dtype)
            for pr in range(npair):
                st[z, pr] = (_bdot(stb[z, pr], mm_ref[z][0, 0, c * npair + pr])
                             + nn_ref[z][0, 0, c * npair + pr])
    for (z, pr), x in st.items():
        s_ref[z, pr] = x
    for gen in hgrn:
        for _ in gen:
            pass


def _rwkv_pre(p_rw, mu, vecs, dvecs, w2cat, a2cat, g2, hones, tri, nctx):
    B, TC, _ = p_rw.shape
    nt = TC // TILE
    hb = TILE // GRID_W
    nhb = TC // GRID_W
    ntr = (TILE // CHUNK) * (RW_HEADS // 2)
    full = lambda a: pl.BlockSpec(a.shape, lambda b, t: (0,) * a.ndim)
    both = pl.BlockSpec((2, 1, TILE, RW_WIDTH), lambda b, t: (0, b, t, 0))
    trans = pl.BlockSpec((2, 1, ntr, PAIR, PAIR), lambda b, t: (0, b, t, 0, 0))
    one = pl.BlockSpec((1, TILE, RW_WIDTH), lambda b, t: (b, t, 0))
    dir_bf = jax.ShapeDtypeStruct((2, B, TC, RW_WIDTH), BF16)
    return pl.pallas_call(
        functools.partial(_rwkv_pre_kernel, nt=nt, nctx=nctx),
        grid=(B, nt),
        in_specs=[pl.BlockSpec((1, TILE, RW_COLS), lambda b, t: (b, t, 0)),
                  pl.BlockSpec((1, GRID_W, RW_COLS), lambda b, t: (b, jnp.maximum(t * hb - 1, 0), 0)),
                  pl.BlockSpec((1, GRID_W, RW_COLS),
                               lambda b, t: (b, jnp.minimum((t + 1) * hb, nhb - 1), 0)),
                  full(mu), full(vecs), full(dvecs), full(w2cat), full(a2cat), full(g2),
                  full(hones), full(tri)],
        out_specs=[both, both, trans, trans, one, one],
        out_shape=[dir_bf, dir_bf,
                   jax.ShapeDtypeStruct((2, B, nt * ntr, PAIR, PAIR), BF16),
                   jax.ShapeDtypeStruct((2, B, nt * ntr, PAIR, PAIR), F32),
                   jax.ShapeDtypeStruct((B, TC, RW_WIDTH), BF16),
                   jax.ShapeDtypeStruct((B, TC, RW_WIDTH), BF16)],
        compiler_params=_params("arbitrary", "arbitrary"),
        name="rwkv7_pre",
    )(p_rw, p_rw, p_rw, mu, vecs, dvecs, w2cat, a2cat, g2, hones, tri)


def _mixer(p_hg, lbtab, stack, rhat, yloc, mm, nn, nctx):
    _, B, TC, _ = rhat.shape
    nt = TC // TILE
    ntr = (TILE // CHUNK) * (RW_HEADS // 2)
    assert HG_WIDTH == RW_WIDTH
    full = lambda a: pl.BlockSpec(a.shape, lambda b, s: (0,) * a.ndim)
    in_specs, operands = [full(lbtab), full(stack)], [lbtab, stack]
    out_specs = []
    for z in range(2):
        tile = functools.partial(_scan_tile, z, nt=nt, nctx=nctx)
        col = lambda c, tile=tile: pl.BlockSpec((1, TILE, HG_WIDTH), lambda b, s: (b, tile(s), c))
        dir_tile = pl.BlockSpec((1, 1, TILE, RW_WIDTH), lambda b, s, z=z, tile=tile: (z, b, tile(s), 0))
        trans = pl.BlockSpec((1, 1, ntr, PAIR, PAIR), lambda b, s, z=z, tile=tile: (z, b, tile(s), 0, 0))
        in_specs += [col(0), col(1 + z), col(3), dir_tile, dir_tile, trans, trans]
        operands += [p_hg, p_hg, p_hg, rhat, yloc, mm, nn]
        out_specs += [col(0), col(0)]
    out_shape = jax.ShapeDtypeStruct((B, TC, RW_WIDTH), BF16)
    return pl.pallas_call(
        _mixer_kernel,
        grid=(B, nt),
        in_specs=in_specs,
        out_specs=out_specs,
        out_shape=[out_shape] * 4,
        scratch_shapes=[pltpu.VMEM((2, HG_HEADS, HG_HEAD, HG_HEAD), F32),
                        pltpu.VMEM((2, RW_HEADS // 2, PAIR, PAIR), F32)],
        compiler_params=_params("arbitrary", "arbitrary"),
        name="mixer_scans",
    )(*operands)


def _merge_kernel(of_ref, ob_ref, og_ref, yf_ref, yb_ref, bon_ref, g_ref, gt_ref, x_ref, mod_ref,
                  hgn_ref, gn_ref, ng_ref, hones_ref, pa_ref, pb_ref, wo_ref, o_ref):
    D = x_ref.shape[-1]
    H = HG_HEAD
    oh = of_ref[0].astype(F32) + ob_ref[0].astype(F32)
    og = og_ref[0]
    hgn = hgn_ref[...]
    parts = []
    for h in range(HG_HEADS):
        ls = slice(h * H, (h + 1) * H)
        parts.append(_rms(oh[:, ls], hgn) * (og[:, ls] * _sigmoid(og[:, ls])))
    o_hg = jnp.concatenate(parts, axis=-1)
    hones = hones_ref[...]
    y = yf_ref[0].astype(F32) + yb_ref[0].astype(F32)
    mean = _dot_exact_rhs(y, hones) * (1.0 / RW_HEAD)
    yc = y - mean
    var = _dot_exact_rhs(yc * yc, hones) * (1.0 / RW_HEAD)
    yn = yc * lax.rsqrt(var + RW_GN_EPS) * gn_ref[0:1, :] + gn_ref[1:2, :]
    o_rw = (yn + bon_ref[0].astype(F32)) * g_ref[0].astype(F32)
    gt = gt_ref[0].astype(F32)
    m = (_sigmoid(gt[:, :D]) * _bdot(o_hg, pa_ref[...])
         + _sigmoid(gt[:, D:]) * _bdot(o_rw, pb_ref[...]))
    yx = _bdot(m, wo_ref[...])
    gate = mod_ref[0, :, 2 * D:3 * D]
    o_ref[0] = x_ref[0] + gate * _rms(yx, ng_ref[...])


def _merge(o_f, o_b, p_hg, y_f, y_b, bon, g, p_gt, xcat, modtab, hgn, gn, ng, hones, pa, pb, wo, t0):
    B, TC, D = xcat.shape
    nt = TC // TILE - t0
    full = lambda a: pl.BlockSpec(a.shape, lambda b, t: (0,) * a.ndim)
    tok = pl.BlockSpec((1, TILE, RW_WIDTH), lambda b, t: (b, t + t0, 0))
    return pl.pallas_call(
        _merge_kernel,
        grid=(B, nt),
        in_specs=[tok, tok,
                  pl.BlockSpec((1, TILE, HG_WIDTH), lambda b, t: (b, t + t0, 4)),
                  tok, tok, tok, tok,
                  pl.BlockSpec((1, TILE, 2 * D), lambda b, t: (b, t + t0, 0)),
                  pl.BlockSpec((1, TILE, D), lambda b, t: (b, t + t0, 0)),
                  pl.BlockSpec((1, 1, modtab.shape[-1]),
                               lambda b, t: (2 * b + jnp.minimum(t + t0, 1), 0, 0)),
                  full(hgn), full(gn), full(ng), full(hones), full(pa), full(pb), full(wo)],
        out_specs=pl.BlockSpec((1, TILE, D), lambda b, t: (b, t, 0)),
        out_shape=jax.ShapeDtypeStruct((B, nt * TILE, D), F32),
        compiler_params=_params("arbitrary", "arbitrary"),
        name="gated_merge",
    )(o_f, o_b, p_hg, y_f, y_b, bon, g, p_gt, xcat, modtab, hgn, gn, ng, hones, pa, pb, wo)


def _ffn_kernel(x_ref, mod_ref, g_ref, w1_ref, w2_ref, o_ref):
    D = x_ref.shape[-1]
    F = w2_ref.shape[0]
    x = x_ref[0]
    h = _rms(x, g_ref[0:1, :]) * (1.0 + mod_ref[0, :, 4 * D:5 * D]) + mod_ref[0, :, 3 * D:4 * D]
    hb = h.astype(BF16)
    gt = jnp.dot(hb, w1_ref[:, 0:F], preferred_element_type=F32)
    up = jnp.dot(hb, w1_ref[:, F:2 * F], preferred_element_type=F32)
    act = (gt * _sigmoid(gt) * up).astype(BF16)
    y = jnp.dot(act, w2_ref[...], preferred_element_type=F32)
    o_ref[0] = x + mod_ref[0, :, 5 * D:6 * D] * _rms(y, g_ref[1:2, :])


def _ffn(x, modtab, g, w1, w2, t0):
    B, TX, D = x.shape
    nt = TX // TILE
    full = lambda a: pl.BlockSpec(a.shape, lambda b, t: (0,) * a.ndim)
    return pl.pallas_call(
        _ffn_kernel,
        grid=(B, nt),
        in_specs=[pl.BlockSpec((1, TILE, D), lambda b, t: (b, t, 0)),
                  pl.BlockSpec((1, 1, modtab.shape[-1]),
                               lambda b, t: (2 * b + jnp.minimum(t + t0, 1), 0, 0)),
                  full(g), full(w1), full(w2)],
        out_specs=pl.BlockSpec((1, TILE, D), lambda b, t: (b, t, 0)),
        out_shape=jax.ShapeDtypeStruct((B, TX, D), F32),
        compiler_params=_params("arbitrary", "arbitrary"),
        name="swiglu_ffn",
    )(x, modtab, g, w1, w2)


def kernel(x, c, ctx, c_ctx, ada_w, ada_b, norm_g, w_in, hg_lb_logits, hg_norm_g, rw_mu, rw_w0, rw_w2,
           rw_a0, rw_a2, rw_g2, rw_kk, rw_ka, rw_rk, rw_gn_g, rw_gn_b, proj_a, proj_b, w_out, ffn_w1,
           ffn_w2):
    B, T, D = x.shape
    LCTX = ctx.shape[1]
    L = ada_w.shape[0]
    assert T % TILE == 0 and LCTX == TILE and TILE % GRID_W == 0 and T // TILE >= 2
    nctx = LCTX // TILE

    rows = -(-(B + 1) // 8) * 8
    cond = jnp.zeros((rows, D), F32).at[:B].set(c).at[B].set(c_ctx)
    mods = _mods(cond, ada_w, ada_b)

    lb_cum = jnp.cumsum(jax.nn.softmax(hg_lb_logits.astype(F32), axis=0), axis=0)
    hg_lb = lb_cum - lb_cum[:1]
    lbtab = jnp.stack([jnp.log(hg_lb), jnp.log1p(-hg_lb), 1.0 - hg_lb], axis=2)

    stack = jnp.asarray(_hgrn_stack(), BF16)
    tri = jnp.asarray(_rw_tri(), BF16)
    hones = jnp.asarray(_head_ones(), BF16)

    zeros = jnp.zeros_like(rw_w2[:, 0])
    w2cat = jnp.stack([jnp.concatenate([rw_w2[:, 0], zeros], axis=1),
                       jnp.concatenate([zeros, rw_w2[:, 1]], axis=1)], axis=1).astype(BF16)
    a2cat = jnp.stack([jnp.concatenate([rw_a2[:, 0], zeros], axis=1),
                       jnp.concatenate([zeros, rw_a2[:, 1]], axis=1)], axis=1).astype(BF16)

    xcat = jnp.concatenate([ctx, x], axis=1)
    for l in range(L):
        last = l == L - 1
        lat = mods[l, :B]
        ctxm = jnp.broadcast_to(mods[l, B], lat.shape)
        modtab = jnp.stack([ctxm, lat], axis=1).reshape(2 * B, 1, 6 * D)
        wl = w_in[l].astype(BF16)
        p_hg, p_rw, p_gt = _inproj(xcat, modtab, norm_g[l, 0:1], wl[:, :HG_COLS],
                                   wl[:, HG_COLS:HG_COLS + RW_COLS], wl[:, HG_COLS + RW_COLS:])
        vecs = jnp.stack([rw_kk[l], rw_ka[l], rw_rk[l].reshape(-1)], axis=0)
        dvecs = jnp.stack([rw_w0[l], rw_a0[l]], axis=1)
        rhat, yloc, mm, nn, bon, g = _rwkv_pre(p_rw, rw_mu[l][None, :], vecs, dvecs, w2cat[l],
                                               a2cat[l], rw_g2[l].astype(BF16), hones, tri, nctx)
        o_f, y_f, o_b, y_b = _mixer(p_hg, lbtab[l], stack, rhat, yloc, mm, nn, nctx)
        t0 = nctx if last else 0
        xm = _merge(o_f, o_b, p_hg, y_f, y_b, bon, g, p_gt, xcat, modtab, hg_norm_g[l][None, :],
                    jnp.stack([rw_gn_g[l], rw_gn_b[l]], axis=0), norm_g[l, 1:2], hones,
                    proj_a[l].astype(BF16), proj_b[l].astype(BF16), w_out[l].astype(BF16), t0)
        xcat = _ffn(xm, modtab, norm_g[l, 2:4], ffn_w1[l].astype(BF16), ffn_w2[l].astype(BF16), t0)
    return xcat
```

```python
import functools

import numpy as np
import jax
import jax.numpy as jnp
from jax import lax
from jax.experimental import pallas as pl
from jax.experimental.pallas import tpu as pltpu

F32 = jnp.float32
BF16 = jnp.bfloat16

NORM_EPS = 1e-6
GRID_W = 64
HG_HEADS = 4
HG_HEAD = 128
HG_WIDTH = HG_HEADS * HG_HEAD
RW_HEADS = 8
RW_HEAD = 64
RW_WIDTH = RW_HEADS * RW_HEAD
RW_LORA = 64
RW_GATE_LORA = 128
RW_GN_EPS = 64e-5
RW_COLS = 3 * RW_WIDTH + 4 * RW_LORA + RW_GATE_LORA
HG_COLS = 5 * HG_WIDTH

TILE = 256
CHUNK = 64
LEVELS = 6
HG_MM_LEVELS = 3
HG_FILL = 3
PAIR = 2 * RW_HEAD
VMEM_LIMIT = 56 * 1024 * 1024


def _bdot(a, b):
    return jnp.dot(a.astype(BF16), b.astype(BF16), preferred_element_type=F32)


def _bdot_nt(a, b):
    return lax.dot_general(a.astype(BF16), b.astype(BF16), (((1,), (1,)), ((), ())),
                           preferred_element_type=F32)


def _bdot_tn(a, b):
    return lax.dot_general(a.astype(BF16), b.astype(BF16), (((0,), (0,)), ((), ())),
                           preferred_element_type=F32)


def _split3(x):
    hi = x.astype(BF16)
    r1 = x - hi.astype(F32)
    mid = r1.astype(BF16)
    lo = (r1 - mid.astype(F32)).astype(BF16)
    return hi, mid, lo


def _dot_exact_lhs(m01, x, terms=2):
    d = lambda t: jnp.dot(m01, t, preferred_element_type=F32)
    return sum(d(t) for t in _split3(x)[:terms])


def _dot_exact_rhs(x, m01, terms=2):
    d = lambda t: jnp.dot(t, m01, preferred_element_type=F32)
    return sum(d(t) for t in _split3(x)[:terms])


def _rms(x, g):
    ms = jnp.mean(x * x, axis=-1, keepdims=True)
    return x * lax.rsqrt(ms + NORM_EPS) * g


def _sigmoid(x):
    return 0.5 * jnp.tanh(0.5 * x) + 0.5


def _params(*sem):
    return pltpu.CompilerParams(dimension_semantics=sem, vmem_limit_bytes=VMEM_LIMIT)


def _mods_kernel(c_ref, w_ref, b_ref, o_ref):
    c = c_ref[...]
    s = c * _sigmoid(c)
    sh, sm, sl = _split3(s)
    wh, wm, wl = _split3(w_ref[0])
    d = lambda a, b: jnp.dot(a, b, preferred_element_type=F32)
    acc = d(sh, wh) + d(sh, wm) + d(sm, wh) + d(sh, wl) + d(sl, wh) + d(sm, wm)
    o_ref[0] = acc + b_ref[0]


def _mods(cond, ada_w, ada_b):
    L, D, N = ada_w.shape
    rows = cond.shape[0]
    tn = 1536
    return pl.pallas_call(
        _mods_kernel,
        grid=(L, N // tn),
        in_specs=[pl.BlockSpec((rows, D), lambda l, n: (0, 0)),
                  pl.BlockSpec((1, D, tn), lambda l, n: (l, 0, n)),
                  pl.BlockSpec((1, 1, tn), lambda l, n: (l, 0, n))],
        out_specs=pl.BlockSpec((1, rows, tn), lambda l, n: (l, 0, n)),
        out_shape=jax.ShapeDtypeStruct((L, rows, N), F32),
        compiler_params=_params("arbitrary", "arbitrary"),
        name="adaln_mods",
    )(cond, ada_w, ada_b.reshape(L, 1, N))


def _inproj_kernel(x_ref, mod_ref, g_ref, whg_ref, wrw_ref, wgt_ref, ohg_ref, orw_ref, ogt_ref):
    D = x_ref.shape[-1]
    y = _rms(x_ref[0], g_ref[...])
    shift = mod_ref[0, :, 0:D]
    scale = mod_ref[0, :, D:2 * D]
    h = (y * (1.0 + scale) + shift).astype(BF16)
    ohg_ref[0] = jnp.dot(h, whg_ref[...], preferred_element_type=F32)
    orw_ref[0] = jnp.dot(h, wrw_ref[...], preferred_element_type=F32)
    ogt_ref[0] = jnp.dot(h, wgt_ref[...], preferred_element_type=F32).astype(ogt_ref.dtype)


def _inproj(xcat, modtab, g, whg, wrw, wgt):
    B, TC, D = xcat.shape
    nt = TC // TILE
    full = lambda a: pl.BlockSpec(a.shape, lambda b, t: (0,) * a.ndim)
    outs = [jax.ShapeDtypeStruct((B, TC, w.shape[1]), dt) for w, dt in ((whg, F32), (wrw, F32), (wgt, BF16))]
    return pl.pallas_call(
        _inproj_kernel,
        grid=(B, nt),
        in_specs=[pl.BlockSpec((1, TILE, D), lambda b, t: (b, t, 0)),
                  pl.BlockSpec((1, 1, modtab.shape[-1]), lambda b, t: (2 * b + jnp.minimum(t, 1), 0, 0)),
                  full(g), full(whg), full(wrw), full(wgt)],
        out_specs=[pl.BlockSpec((1, TILE, o.shape[-1]), lambda b, t: (b, t, 0)) for o in outs],
        out_shape=outs,
        compiler_params=_params("arbitrary", "arbitrary"),
        name="in_proj",
    )(xcat, modtab, g, whg, wrw, wgt)


def _scan_tile(d, s, nt, nctx):
    bwd = jnp.where(s < nctx, nctx - 1 - s, nt - 1 - (s - nctx))
    return jnp.where(d == 0, s, bwd)


def _hgrn_stack():
    c = CHUNK
    out = np.zeros((2, (HG_MM_LEVELS + 1) * c, c), np.float32)
    for d in range(2):
        pi = np.arange(c) if d == 0 else c - 1 - np.arange(c)
        out[d, :c] = pi[None, :] <= pi[:, None]
        for lv in range(1, HG_MM_LEVELS + 1):
            m = 1 << lv
            mid = (pi // m) * m + m // 2
            late = (pi % m) >= m // 2
            sum_late = (pi[None, :] >= mid[:, None]) & (pi[None, :] <= pi[:, None])
            sum_early = (pi[None, :] > pi[:, None]) & (pi[None, :] <= mid[:, None] - 1)
            out[d, lv * c:(lv + 1) * c] = np.where(late[:, None], sum_late, sum_early)
    return out


def _hgrn_steps(q_ref, f_ref, i_ref, lb_ref, stack_ref, o_ref, st_ref, d):
    C = CHUNK
    H = HG_HEAD

    log_lb = lb_ref[d, 0:1, :]
    log1m_lb = lb_ref[d, 1:2, :]
    one_m_lb = lb_ref[d, 2:3, :]
    stack = stack_ref[d]

    row = lax.broadcasted_iota(jnp.int32, (TILE, HG_WIDTH), 0) & (C - 1)
    pi = row if d == 0 else C - 1 - row
    ii = lax.broadcasted_iota(jnp.int32, (C, C), 0)
    jj = lax.broadcasted_iota(jnp.int32, (C, C), 1)
    valid = {}
    for lv in range(1, LEVELS + 1):
        hi = (ii if d == 0 else C - 1 - ii) >> (lv - 1)
        hj = (jj if d == 0 else C - 1 - jj) >> (lv - 1)
        valid[lv] = (hi - hj + ((hj & 1) << 8)) == 1
    nchunk = TILE // C
    hsl = [slice(h * H, (h + 1) * H) for h in range(HG_HEADS)]
    ntdot = lambda a, b: lax.dot_general(a, b, (((1,), (1,)), ((), ())), preferred_element_type=F32)

    crows = [slice(c * C, (c + 1) * C) for c in range(nchunk)]
    cat0 = lambda xs: jnp.concatenate(xs, axis=0)
    xf = f_ref[0]
    nlog2e = -1.0 / float(np.log(2.0))
    lsig = jnp.minimum(xf, 0.0) - jnp.log(1.0 + jnp.exp2(jnp.abs(xf) * nlog2e))
    t2 = log1m_lb + lsig
    mx = jnp.maximum(log_lb, t2)
    logf = mx + jnp.log(1.0 + jnp.exp2(jnp.abs(log_lb - t2) * nlog2e))
    half_k = 0.5 * one_m_lb
    kin = half_k - half_k * jnp.tanh(0.5 * xf)
    qr = q_ref[0]
    half_s = 0.5 * HG_HEAD ** -0.5
    q = qr * (jnp.tanh(0.5 * qr) * half_s + half_s)
    vb = i_ref[0].astype(BF16)

    lf_terms = _split3(logf)[:2]
    e_chunk = [sum(jnp.dot(stack, t[rs], preferred_element_type=F32) for t in lf_terms)
               for rs in crows]
    cs = cat0([e[0:C] for e in e_chunk])
    tot = [jnp.sum(logf[rs], axis=0, keepdims=True) for rs in crows]
    qb = q.astype(BF16)
    kb = kin.astype(BF16)
    scores = {(c, h): jnp.where(ii == jj, ntdot(qb[rs, ls], kb[rs, ls]), 0.0)
              for c, rs in enumerate(crows) for h, ls in enumerate(hsl)}
    yield
    for lv in range(1, LEVELS + 1):
        m = 1 << lv
        late = ((pi >> (lv - 1)) & 1) == 1
        if lv <= HG_MM_LEVELS:
            e = cat0([ec[lv * C:(lv + 1) * C] for ec in e_chunk])
        else:
            at_row = [b * m + m // 2 - (1 if d == 0 else 0) for b in range(TILE // m)]
            cmid = cat0([jnp.broadcast_to(cs[r:r + 1], (m, HG_WIDTH)) for r in at_row])
            e = jnp.where(late, cs - cmid, cmid - cs)
        qk = (jnp.where(late, q, kin) * jnp.exp(e)).astype(BF16)
        scores = {(c, h): jnp.where(valid[lv], ntdot(qk[crows[c], hsl[h]], qk[crows[c], hsl[h]]), sc)
                  for (c, h), sc in scores.items()}
        yield
    tot_rows = cat0([jnp.broadcast_to(t, (C, HG_WIDTH)) for t in tot])
    q_in = (q * jnp.exp(cs)).astype(BF16)
    k_end = (kin * jnp.exp(tot_rows - cs)).astype(BF16)
    o_loc = {(c, h): jnp.dot(scores[c, h].astype(BF16), vb[crows[c], hsl[h]],
                             preferred_element_type=F32) for c, h in scores}
    kv = {(c, h): lax.dot_general(vb[crows[c], hsl[h]], k_end[crows[c], hsl[h]],
                                  (((0,), (0,)), ((), ())), preferred_element_type=F32)
          for c, h in scores}
    yield
    st = [st_ref[d, h] for h in range(HG_HEADS)]
    for step in range(nchunk):
        c = step if d == 0 else nchunk - 1 - step
        dec_end = jnp.exp(tot[c])
        for h, ls in enumerate(hsl):
            o_ref[0, crows[c], ls] = (ntdot(q_in[crows[c], ls], st[h].astype(BF16))
                                      + o_loc[c, h]).astype(o_ref.dtype)
        st = [st[h] * dec_end[:, ls] + kv[c, h] for h, ls in enumerate(hsl)]
        yield
    for h in range(HG_HEADS):
        st_ref[d, h] = st[h]


def _rw_tri():
    out = np.zeros((2, TILE, TILE), np.float32)
    idx = np.arange(TILE)
    same = (idx[:, None] // CHUNK) == (idx[None, :] // CHUNK)
    out[0] = same & (idx[None, :] <= idx[:, None])
    out[1] = same & (idx[None, :] >= idx[:, None])
    return out


def _head_ones():
    idx = np.arange(RW_WIDTH)
    return ((idx[:, None] // RW_HEAD) == (idx[None, :] // RW_HEAD)).astype(np.float32)


def _rwkv_pre_kernel(cur_ref, up_ref, dn_ref, mu_ref, vec_ref, dvec_ref, w2_ref, a2_ref, g2_ref,
                     hones_ref, tri_ref, rhat_ref, yloc_ref, mm_ref, nn_ref, bon_ref, g_ref,
                     *, nt, nctx):
    t = pl.program_id(1)
    C = CHUNK
    W = RW_WIDTH

    p = cur_ref[0]
    is_ctx = t < nctx
    rowi = lax.broadcasted_iota(jnp.int32, p.shape, 0)
    lane = lax.broadcasted_iota(jnp.int32, p.shape, 1)
    prev1 = pltpu.roll(p, 1, 0)
    next1 = pltpu.roll(p, TILE - 1, 0)
    seg = jnp.where(is_ctx, TILE - 1, GRID_W - 1)
    prev1 = jnp.where((rowi & seg) == 0, 0.0, prev1)
    next1 = jnp.where((rowi & seg) == seg, 0.0, next1)
    up_ok = t > nctx
    dn_ok = jnp.logical_and(t >= nctx, t < nt - 1)
    up = jnp.concatenate([jnp.where(up_ok, up_ref[0], 0.0), p[:TILE - GRID_W]], axis=0)
    dn = jnp.concatenate([p[GRID_W:], jnp.where(dn_ok, dn_ref[0], 0.0)], axis=0)
    cls = lane & 3
    sh_even = jnp.where(cls == 0, prev1, jnp.where(is_ctx, prev1, up))
    sh_odd = jnp.where(cls == 1, next1, jnp.where(is_ctx, next1, dn))
    shifted = jnp.where((cls & 1) == 0, sh_even, sh_odd)
    pf = p + mu_ref[...] * (shifted - p)

    r = pf[:, 0:W]
    k = pf[:, W:2 * W]
    v = pf[:, 2 * W:3 * W]
    o3 = 3 * W
    wd = pf[:, o3:o3 + 2 * RW_LORA]
    ad = pf[:, o3 + 2 * RW_LORA:o3 + 4 * RW_LORA]
    gd = pf[:, o3 + 4 * RW_LORA:o3 + 4 * RW_LORA + RW_GATE_LORA]

    k_k = vec_ref[0:1, :]
    k_a = vec_ref[1:2, :]
    r_k = vec_ref[2:3, :]
    hones = hones_ref[...]

    kkr = k * k_k
    ss = _dot_exact_rhs(kkr * kkr, hones)
    kk = kkr * lax.rsqrt(jnp.maximum(ss, 1e-24))
    g_ref[0] = _bdot(_sigmoid(gd), g2_ref[...]).astype(g_ref.dtype)
    vb = v.astype(BF16)
    twd = jnp.tanh(wd).astype(BF16)
    adb = ad.astype(BF16)
    ksum = None
    dirs = (0, 1)
    nchunk = TILE // C
    atd, rtd, btd, ktd, gamd = {}, {}, {}, {}, {}
    for dd in dirs:
        lw = dvec_ref[dd, 0:1, :] + jnp.dot(twd, w2_ref[dd], preferred_element_type=F32)
        ld = -float(np.exp(-0.5)) * _sigmoid(lw)
        a = _sigmoid(dvec_ref[dd, 1:2, :] + jnp.dot(adb, a2_ref[dd], preferred_element_type=F32))
        kdir = k * (1.0 + (a - 1.0) * k_a)
        ksum = kdir if ksum is None else ksum + kdir
        cs = _dot_exact_lhs(tri_ref[dd], ld)
        einv = jnp.exp(-cs)
        atd[dd] = (-kk * jnp.exp(cs - ld)).astype(BF16)
        rtd[dd] = (r * jnp.exp(cs)).astype(BF16)
        btd[dd] = (kk * a * einv).astype(BF16)
        ktd[dd] = (kdir * einv).astype(BF16)
        gamd[dd] = [jnp.exp(jnp.sum(ld[c * C:(c + 1) * C], axis=0, keepdims=True))
                    for c in range(nchunk)]
    bon_ref[0] = (_dot_exact_rhs(r * ksum * r_k, hones) * v).astype(bon_ref.dtype)

    pi_ = lax.broadcasted_iota(jnp.int32, (PAIR, PAIR), 0)
    pj_ = lax.broadcasted_iota(jnp.int32, (PAIR, PAIR), 1)
    same_head = (pi_ < RW_HEAD) == (pj_ < RW_HEAD)
    eye_pair = (pi_ == pj_).astype(F32)
    head0 = lax.broadcasted_iota(jnp.int32, (C, PAIR), 1) < RW_HEAD

    ii4 = lax.broadcasted_iota(jnp.int32, (C, 2 * PAIR), 0)
    jj4 = lax.broadcasted_iota(jnp.int32, (C, 2 * PAIR), 1) & (C - 1)
    before4 = [jj4 < ii4, jj4 > ii4]
    upto4 = [jj4 <= ii4, jj4 >= ii4]
    eye2 = (lax.broadcasted_iota(jnp.int32, (C, PAIR), 0)
            == (lax.broadcasted_iota(jnp.int32, (C, PAIR), 1) & (C - 1))).astype(F32)
    npair = RW_HEADS // 2
    lsl = [slice(pr * PAIR, (pr + 1) * PAIR) for pr in range(npair)]
    pairs = [(z, c, pr) for z in dirs for c in range(nchunk) for pr in range(npair)]
    fdot = lambda a, b: jnp.dot(a, b, preferred_element_type=F32)
    cat0 = lambda xs: jnp.concatenate(xs, axis=0)

    def bd(x):
        x = x.astype(F32)
        return cat0([jnp.where(head0, x, 0.0), jnp.where(head0, 0.0, x)]).astype(BF16)

    rows = lambda c: slice(c * C, (c + 1) * C)
    at = {(z, c, pr): atd[z][rows(c), lsl[pr]] for z, c, pr in pairs}
    rt = {(z, c, pr): rtd[z][rows(c), lsl[pr]] for z, c, pr in pairs}
    bt = {(z, c, pr): btd[z][rows(c), lsl[pr]] for z, c, pr in pairs}
    kt = {(z, c, pr): ktd[z][rows(c), lsl[pr]] for z, c, pr in pairs}
    vv = {(z, c, pr): vb[rows(c), lsl[pr]] for z, c, pr in pairs}
    gam = {(z, c, pr): gamd[z][c][:, lsl[pr]] for z, c, pr in pairs}
    bk = {pq: cat0([bt[pq], kt[pq]]) for pq in pairs}
    bdv = {pq: bd(vv[pq]) for pq in pairs}
    top, bot = {}, {}
    for pq in pairs:
        lhs = cat0([at[pq], rt[pq]])
        x = _bdot_nt(lhs, cat0([bd(bt[pq]), bd(kt[pq])]))
        top[pq] = jnp.where(before4[pq[0]], x[:C], 0.0)
        bot[pq] = jnp.where(upto4[pq[0]], x[C:], 0.0).astype(BF16)
    nf = {pq: top[pq][:, :PAIR] for pq in pairs}
    tinv = {pq: eye2 + nf[pq] for pq in pairs}
    pw = {pq: fdot(nf[pq].astype(BF16), bd(nf[pq])) for pq in pairs}
    w = {pq: fdot(top[pq][:, PAIR:].astype(BF16), bdv[pq]) for pq in pairs}
    for _ in range(LEVELS - 2):
        res = {pq: fdot(cat0([tinv[pq], pw[pq]]).astype(BF16), bd(pw[pq])) for pq in pairs}
        tinv = {pq: tinv[pq] + res[pq][:C] for pq in pairs}
        pw = {pq: res[pq][C:] for pq in pairs}
    tinv = {pq: tinv[pq] + fdot(tinv[pq].astype(BF16), bd(pw[pq])) for pq in pairs}
    tb = {pq: tinv[pq].astype(BF16) for pq in pairs}
    ahf = {pq: fdot(tb[pq], bd(at[pq])) for pq in pairs}
    uvf = {pq: fdot(tb[pq], bd(w[pq])) for pq in pairs}
    for z, c, pr in pairs:
        pq = (z, c, pr)
        rhat_ref[z, 0, rows(c), lsl[pr]] = (
            rt[pq] + fdot(bot[pq][:, :PAIR], bd(ahf[pq]))).astype(rhat_ref.dtype)
        yloc_ref[z, 0, rows(c), lsl[pr]] = fdot(
            bot[pq], cat0([bd(uvf[pq]), bdv[pq]])).astype(yloc_ref.dtype)
    for z, c, pr in pairs:
        pq = (z, c, pr)
        mm_ref[z, 0, c * npair + pr] = (
            jnp.where(same_head, eye_pair + _bdot_tn(ahf[pq], bk[pq][:C]), 0.0) * gam[pq]
        ).astype(mm_ref.dtype)
        nn_ref[z, 0, c * npair + pr] = jnp.where(
            same_head, _bdot_tn(cat0([uvf[pq].astype(BF16), vv[pq]]), bk[pq]), 0.0) * gam[pq]


def _mixer_kernel(lb_ref, stack_ref, *refs):
    nin = 7
    per_dir = [refs[z * nin:(z + 1) * nin] for z in range(2)]
    o_refs = [refs[2 * nin], refs[2 * nin + 2]]
    y_refs = [refs[2 * nin + 1], refs[2 * nin + 3]]
    st_ref, s_ref = refs[2 * nin + 4:]
    s = pl.program_id(1)
    C = CHUNK
    nchunk = TILE // C
    npair = RW_HEADS // 2
    dirs = (0, 1)
    lsl = [slice(pr * PAIR, (pr + 1) * PAIR) for pr in range(npair)]

    @pl.when(s == 0)
    def _():
        st_ref[...] = jnp.zeros_like(st_ref)
        s_ref[...] = jnp.zeros_like(s_ref)

    hgrn = [_hgrn_steps(*per_dir[z][:3], lb_ref, stack_ref, o_refs[z], st_ref, z) for z in dirs]

    def fill(n):
        for _ in range(n):
            for gen in hgrn:
                next(gen, None)

    rhat_ref, yloc_ref, mm_ref, nn_ref = zip(*[per_dir[z][3:] for z in dirs])
    st = {(z, pr): s_ref[z, pr] for z in dirs for pr in range(npair)}
    for u in range(nchunk):
        fill(HG_FILL)
        stb = {k: x.astype(BF16) for k, x in st.items()}
        for z in dirs:
            c = u if z == 0 else nchunk - 1 - u
            for pr in range(npair):
                y_refs[z][0, c * C:(c + 1) * C, lsl[pr]] = (
                    _bdot_nt(rhat_ref[z][0, 0, c * C:(c + 1) * C, lsl[pr]], stb[z, pr])
                    + yloc_ref[z][0, 0, c * C:(c + 1) * C, lsl[pr]].astype(F32)
                ).astype(y_refs[z].dtype)
            for pr in range(npair):
                st[z, pr] = (_bdot(stb[z, pr], mm_ref[z][0, 0, c * npair + pr])
                             + nn_ref[z][0, 0, c * npair + pr])
    for (z, pr), x in st.items():
        s_ref[z, pr] = x
    for gen in hgrn:
        for _ in gen:
            pass


def _rwkv_pre(p_rw, mu, vecs, dvecs, w2cat, a2cat, g2, hones, tri, nctx):
    B, TC, _ = p_rw.shape
    nt = TC // TILE
    hb = TILE // GRID_W
    nhb = TC // GRID_W
    ntr = (TILE // CHUNK) * (RW_HEADS // 2)
    full = lambda a: pl.BlockSpec(a.shape, lambda b, t: (0,) * a.ndim)
    both = pl.BlockSpec((2, 1, TILE, RW_WIDTH), lambda b, t: (0, b, t, 0))
    trans = pl.BlockSpec((2, 1, ntr, PAIR, PAIR), lambda b, t: (0, b, t, 0, 0))
    one = pl.BlockSpec((1, TILE, RW_WIDTH), lambda b, t: (b, t, 0))
    dir_bf = jax.ShapeDtypeStruct((2, B, TC, RW_WIDTH), BF16)
    return pl.pallas_call(
        functools.partial(_rwkv_pre_kernel, nt=nt, nctx=nctx),
        grid=(B, nt),
        in_specs=[pl.BlockSpec((1, TILE, RW_COLS), lambda b, t: (b, t, 0)),
                  pl.BlockSpec((1, GRID_W, RW_COLS), lambda b, t: (b, jnp.maximum(t * hb - 1, 0), 0)),
                  pl.BlockSpec((1, GRID_W, RW_COLS),
                               lambda b, t: (b, jnp.minimum((t + 1) * hb, nhb - 1), 0)),
                  full(mu), full(vecs), full(dvecs), full(w2cat), full(a2cat), full(g2),
                  full(hones), full(tri)],
        out_specs=[both, both, trans, trans, one, one],
        out_shape=[dir_bf, dir_bf,
                   jax.ShapeDtypeStruct((2, B, nt * ntr, PAIR, PAIR), BF16),
                   jax.ShapeDtypeStruct((2, B, nt * ntr, PAIR, PAIR), F32),
                   jax.ShapeDtypeStruct((B, TC, RW_WIDTH), BF16),
                   jax.ShapeDtypeStruct((B, TC, RW_WIDTH), BF16)],
        compiler_params=_params("arbitrary", "arbitrary"),
        name="rwkv7_pre",
    )(p_rw, p_rw, p_rw, mu, vecs, dvecs, w2cat, a2cat, g2, hones, tri)


def _mixer(p_hg, lbtab, stack, rhat, yloc, mm, nn, nctx):
    _, B, TC, _ = rhat.shape
    nt = TC // TILE
    ntr = (TILE // CHUNK) * (RW_HEADS // 2)
    assert HG_WIDTH == RW_WIDTH
    full = lambda a: pl.BlockSpec(a.shape, lambda b, s: (0,) * a.ndim)
    in_specs, operands = [full(lbtab), full(stack)], [lbtab, stack]
    out_specs = []
    for z in range(2):
        tile = functools.partial(_scan_tile, z, nt=nt, nctx=nctx)
        col = lambda c, tile=tile: pl.BlockSpec((1, TILE, HG_WIDTH), lambda b, s: (b, tile(s), c))
        dir_tile = pl.BlockSpec((1, 1, TILE, RW_WIDTH), lambda b, s, z=z, tile=tile: (z, b, tile(s), 0))
        trans = pl.BlockSpec((1, 1, ntr, PAIR, PAIR), lambda b, s, z=z, tile=tile: (z, b, tile(s), 0, 0))
        in_specs += [col(0), col(1 + z), col(3), dir_tile, dir_tile, trans, trans]
        operands += [p_hg, p_hg, p_hg, rhat, yloc, mm, nn]
        out_specs += [col(0), col(0)]
    out_shape = jax.ShapeDtypeStruct((B, TC, RW_WIDTH), BF16)
    return pl.pallas_call(
        _mixer_kernel,
        grid=(B, nt),
        in_specs=in_specs,
        out_specs=out_specs,
        out_shape=[out_shape] * 4,
        scratch_shapes=[pltpu.VMEM((2, HG_HEADS, HG_HEAD, HG_HEAD), F32),
                        pltpu.VMEM((2, RW_HEADS // 2, PAIR, PAIR), F32)],
        compiler_params=_params("arbitrary", "arbitrary"),
        name="mixer_scans",
    )(*operands)


def _merge_kernel(of_ref, ob_ref, og_ref, yf_ref, yb_ref, bon_ref, g_ref, gt_ref, x_ref, mod_ref,
                  hgn_ref, gn_ref, ng_ref, hones_ref, pa_ref, pb_ref, wo_ref, o_ref):
    D = x_ref.shape[-1]
    H = HG_HEAD
    oh = of_ref[0].astype(F32) + ob_ref[0].astype(F32)
    og = og_ref[0]
    hgn = hgn_ref[...]
    parts = []
    for h in range(HG_HEADS):
        ls = slice(h * H, (h + 1) * H)
        parts.append(_rms(oh[:, ls], hgn) * (og[:, ls] * _sigmoid(og[:, ls])))
    o_hg = jnp.concatenate(parts, axis=-1)
    hones = hones_ref[...]
    y = yf_ref[0].astype(F32) + yb_ref[0].astype(F32)
    mean = _dot_exact_rhs(y, hones) * (1.0 / RW_HEAD)
    yc = y - mean
    var = _dot_exact_rhs(yc * yc, hones) * (1.0 / RW_HEAD)
    yn = yc * lax.rsqrt(var + RW_GN_EPS) * gn_ref[0:1, :] + gn_ref[1:2, :]
    o_rw = (yn + bon_ref[0].astype(F32)) * g_ref[0].astype(F32)
    gt = gt_ref[0].astype(F32)
    m = (_sigmoid(gt[:, :D]) * _bdot(o_hg, pa_ref[...])
         + _sigmoid(gt[:, D:]) * _bdot(o_rw, pb_ref[...]))
    yx = _bdot(m, wo_ref[...])
    gate = mod_ref[0, :, 2 * D:3 * D]
    o_ref[0] = x_ref[0] + gate * _rms(yx, ng_ref[...])


def _merge(o_f, o_b, p_hg, y_f, y_b, bon, g, p_gt, xcat, modtab, hgn, gn, ng, hones, pa, pb, wo, t0):
    B, TC, D = xcat.shape
    nt = TC // TILE - t0
    full = lambda a: pl.BlockSpec(a.shape, lambda b, t: (0,) * a.ndim)
    tok = pl.BlockSpec((1, TILE, RW_WIDTH), lambda b, t: (b, t + t0, 0))
    return pl.pallas_call(
        _merge_kernel,
        grid=(B, nt),
        in_specs=[tok, tok,
                  pl.BlockSpec((1, TILE, HG_WIDTH), lambda b, t: (b, t + t0, 4)),
                  tok, tok, tok, tok,
                  pl.BlockSpec((1, TILE, 2 * D), lambda b, t: (b, t + t0, 0)),
                  pl.BlockSpec((1, TILE, D), lambda b, t: (b, t + t0, 0)),
                  pl.BlockSpec((1, 1, modtab.shape[-1]),
                               lambda b, t: (2 * b + jnp.minimum(t + t0, 1), 0, 0)),
                  full(hgn), full(gn), full(ng), full(hones), full(pa), full(pb), full(wo)],
        out_specs=pl.BlockSpec((1, TILE, D), lambda b, t: (b, t, 0)),
        out_shape=jax.ShapeDtypeStruct((B, nt * TILE, D), F32),
        compiler_params=_params("arbitrary", "arbitrary"),
        name="gated_merge",
    )(o_f, o_b, p_hg, y_f, y_b, bon, g, p_gt, xcat, modtab, hgn, gn, ng, hones, pa, pb, wo)


def _ffn_kernel(x_ref, mod_ref, g_ref, w1_ref, w2_ref, o_ref):
    D = x_ref.shape[-1]
    F = w2_ref.shape[0]
    x = x_ref[0]
    h = _rms(x, g_ref[0:1, :]) * (1.0 + mod_ref[0, :, 4 * D:5 * D]) + mod_ref[0, :, 3 * D:4 * D]
    hb = h.astype(BF16)
    gt = jnp.dot(hb, w1_ref[:, 0:F], preferred_element_type=F32)
    up = jnp.dot(hb, w1_ref[:, F:2 * F], preferred_element_type=F32)
    act = (gt * _sigmoid(gt) * up).astype(BF16)
    y = jnp.dot(act, w2_ref[...], preferred_element_type=F32)
    o_ref[0] = x + mod_ref[0, :, 5 * D:6 * D] * _rms(y, g_ref[1:2, :])


def _ffn(x, modtab, g, w1, w2, t0):
    B, TX, D = x.shape
    nt = TX // TILE
    full = lambda a: pl.BlockSpec(a.shape, lambda b, t: (0,) * a.ndim)
    return pl.pallas_call(
        _ffn_kernel,
        grid=(B, nt),
        in_specs=[pl.BlockSpec((1, TILE, D), lambda b, t: (b, t, 0)),
                  pl.BlockSpec((1, 1, modtab.shape[-1]),
                               lambda b, t: (2 * b + jnp.minimum(t + t0, 1), 0, 0)),
                  full(g), full(w1), full(w2)],
        out_specs=pl.BlockSpec((1, TILE, D), lambda b, t: (b, t, 0)),
        out_shape=jax.ShapeDtypeStruct((B, TX, D), F32),
        compiler_params=_params("arbitrary", "arbitrary"),
        name="swiglu_ffn",
    )(x, modtab, g, w1, w2)


def kernel(x, c, ctx, c_ctx, ada_w, ada_b, norm_g, w_in, hg_lb_logits, hg_norm_g, rw_mu, rw_w0, rw_w2,
           rw_a0, rw_a2, rw_g2, rw_kk, rw_ka, rw_rk, rw_gn_g, rw_gn_b, proj_a, proj_b, w_out, ffn_w1,
           ffn_w2):
    B, T, D = x.shape
    LCTX = ctx.shape[1]
    L = ada_w.shape[0]
    assert T % TILE == 0 and LCTX == TILE and TILE % GRID_W == 0 and T // TILE >= 2
    nctx = LCTX // TILE

    rows = -(-(B + 1) // 8) * 8
    cond = jnp.zeros((rows, D), F32).at[:B].set(c).at[B].set(c_ctx)
    mods = _mods(cond, ada_w, ada_b)

    lb_cum = jnp.cumsum(jax.nn.softmax(hg_lb_logits.astype(F32), axis=0), axis=0)
    hg_lb = lb_cum - lb_cum[:1]
    lbtab = jnp.stack([jnp.log(hg_lb), jnp.log1p(-hg_lb), 1.0 - hg_lb], axis=2)

    stack = jnp.asarray(_hgrn_stack(), BF16)
    tri = jnp.asarray(_rw_tri(), BF16)
    hones = jnp.asarray(_head_ones(), BF16)

    zeros = jnp.zeros_like(rw_w2[:, 0])
    w2cat = jnp.stack([jnp.concatenate([rw_w2[:, 0], zeros], axis=1),
                       jnp.concatenate([zeros, rw_w2[:, 1]], axis=1)], axis=1).astype(BF16)
    a2cat = jnp.stack([jnp.concatenate([rw_a2[:, 0], zeros], axis=1),
                       jnp.concatenate([zeros, rw_a2[:, 1]], axis=1)], axis=1).astype(BF16)

    xcat = jnp.concatenate([ctx, x], axis=1)
    for l in range(L):
        last = l == L - 1
        lat = mods[l, :B]
        ctxm = jnp.broadcast_to(mods[l, B], lat.shape)
        modtab = jnp.stack([ctxm, lat], axis=1).reshape(2 * B, 1, 6 * D)
        wl = w_in[l].astype(BF16)
        p_hg, p_rw, p_gt = _inproj(xcat, modtab, norm_g[l, 0:1], wl[:, :HG_COLS],
                                   wl[:, HG_COLS:HG_COLS + RW_COLS], wl[:, HG_COLS + RW_COLS:])
        vecs = jnp.stack([rw_kk[l], rw_ka[l], rw_rk[l].reshape(-1)], axis=0)
        dvecs = jnp.stack([rw_w0[l], rw_a0[l]], axis=1)
        rhat, yloc, mm, nn, bon, g = _rwkv_pre(p_rw, rw_mu[l][None, :], vecs, dvecs, w2cat[l],
                                               a2cat[l], rw_g2[l].astype(BF16), hones, tri, nctx)
        o_f, y_f, o_b, y_b = _mixer(p_hg, lbtab[l], stack, rhat, yloc, mm, nn, nctx)
        t0 = nctx if last else 0
        xm = _merge(o_f, o_b, p_hg, y_f, y_b, bon, g, p_gt, xcat, modtab, hg_norm_g[l][None, :],
                    jnp.stack([rw_gn_g[l], rw_gn_b[l]], axis=0), norm_g[l, 1:2], hones,
                    proj_a[l].astype(BF16), proj_b[l].astype(BF16), w_out[l].astype(BF16), t0)
        xcat = _ffn(xm, modtab, norm_g[l, 2:4], ffn_w1[l].astype(BF16), ffn_w2[l].astype(BF16), t0)
    return xcat
```

```python
import functools

import numpy as np
import jax
import jax.numpy as jnp
from jax import lax
from jax.experimental import pallas as pl
from jax.experimental.pallas import tpu as pltpu

F32 = jnp.float32
BF16 = jnp.bfloat16

NORM_EPS = 1e-6
GRID_W = 64
HG_HEADS = 4
HG_HEAD = 128
HG_WIDTH = HG_HEADS * HG_HEAD
RW_HEADS = 8
RW_HEAD = 64
RW_WIDTH = RW_HEADS * RW_HEAD
RW_LORA = 64
RW_GATE_LORA = 128
RW_GN_EPS = 64e-5
RW_COLS = 3 * RW_WIDTH + 4 * RW_LORA + RW_GATE_LORA
HG_COLS = 5 * HG_WIDTH

TILE = 256
CHUNK = 64
LEVELS = 6
HG_MM_LEVELS = 3
HG_FILL = 3
PAIR = 2 * RW_HEAD
VMEM_LIMIT = 56 * 1024 * 1024


def _bdot(a, b):
    return jnp.dot(a.astype(BF16), b.astype(BF16), preferred_element_type=F32)


def _bdot_nt(a, b):
    return lax.dot_general(a.astype(BF16), b.astype(BF16), (((1,), (1,)), ((), ())),
                           preferred_element_type=F32)


def _bdot_tn(a, b):
    return lax.dot_general(a.astype(BF16), b.astype(BF16), (((0,), (0,)), ((), ())),
                           preferred_element_type=F32)


def _split3(x):
    hi = x.astype(BF16)
    r1 = x - hi.astype(F32)
    mid = r1.astype(BF16)
    lo = (r1 - mid.astype(F32)).astype(BF16)
    return hi, mid, lo


def _dot_exact_lhs(m01, x, terms=2):
    d = lambda t: jnp.dot(m01, t, preferred_element_type=F32)
    return sum(d(t) for t in _split3(x)[:terms])


def _dot_exact_rhs(x, m01, terms=2):
    d = lambda t: jnp.dot(t, m01, preferred_element_type=F32)
    return sum(d(t) for t in _split3(x)[:terms])


def _rms(x, g):
    ms = jnp.mean(x * x, axis=-1, keepdims=True)
    return x * lax.rsqrt(ms + NORM_EPS) * g


def _sigmoid(x):
    return 0.5 * jnp.tanh(0.5 * x) + 0.5


def _params(*sem):
    return pltpu.CompilerParams(dimension_semantics=sem, vmem_limit_bytes=VMEM_LIMIT)


def _mods_kernel(c_ref, w_ref, b_ref, o_ref):
    c = c_ref[...]
    s = c * _sigmoid(c)
    sh, sm, sl = _split3(s)
    wh, wm, wl = _split3(w_ref[0])
    d = lambda a, b: jnp.dot(a, b, preferred_element_type=F32)
    acc = d(sh, wh) + d(sh, wm) + d(sm, wh) + d(sh, wl) + d(sl, wh) + d(sm, wm)
    o_ref[0] = acc + b_ref[0]


def _mods(cond, ada_w, ada_b):
    L, D, N = ada_w.shape
    rows = cond.shape[0]
    tn = 1536
    return pl.pallas_call(
        _mods_kernel,
        grid=(L, N // tn),
        in_specs=[pl.BlockSpec((rows, D), lambda l, n: (0, 0)),
                  pl.BlockSpec((1, D, tn), lambda l, n: (l, 0, n)),
                  pl.BlockSpec((1, 1, tn), lambda l, n: (l, 0, n))],
        out_specs=pl.BlockSpec((1, rows, tn), lambda l, n: (l, 0, n)),
        out_shape=jax.ShapeDtypeStruct((L, rows, N), F32),
        compiler_params=_params("arbitrary", "arbitrary"),
        name="adaln_mods",
    )(cond, ada_w, ada_b.reshape(L, 1, N))


def _inproj_kernel(x_ref, mod_ref, g_ref, whg_ref, wrw_ref, wgt_ref, ohg_ref, orw_ref, ogt_ref):
    D = x_ref.shape[-1]
    y = _rms(x_ref[0], g_ref[...])
    shift = mod_ref[0, :, 0:D]
    scale = mod_ref[0, :, D:2 * D]
    h = (y * (1.0 + scale) + shift).astype(BF16)
    ohg_ref[0] = jnp.dot(h, whg_ref[...], preferred_element_type=F32)
    orw_ref[0] = jnp.dot(h, wrw_ref[...], preferred_element_type=F32)
    ogt_ref[0] = jnp.dot(h, wgt_ref[...], preferred_element_type=F32).astype(ogt_ref.dtype)


def _inproj(xcat, modtab, g, whg, wrw, wgt):
    B, TC, D = xcat.shape
    nt = TC // TILE
    full = lambda a: pl.BlockSpec(a.shape, lambda b, t: (0,) * a.ndim)
    outs = [jax.ShapeDtypeStruct((B, TC, w.shape[1]), dt) for w, dt in ((whg, F32), (wrw, F32), (wgt, BF16))]
    return pl.pallas_call(
        _inproj_kernel,
        grid=(B, nt),
        in_specs=[pl.BlockSpec((1, TILE, D), lambda b, t: (b, t, 0)),
                  pl.BlockSpec((1, 1, modtab.shape[-1]), lambda b, t: (2 * b + jnp.minimum(t, 1), 0, 0)),
                  full(g), full(whg), full(wrw), full(wgt)],
        out_specs=[pl.BlockSpec((1, TILE, o.shape[-1]), lambda b, t: (b, t, 0)) for o in outs],
        out_shape=outs,
        compiler_params=_params("arbitrary", "arbitrary"),
        name="in_proj",
    )(xcat, modtab, g, whg, wrw, wgt)


def _scan_tile(d, s, nt, nctx):
    bwd = jnp.where(s < nctx, nctx - 1 - s, nt - 1 - (s - nctx))
    return jnp.where(d == 0, s, bwd)


def _hgrn_stack():
    c = CHUNK
    out = np.zeros((2, (HG_MM_LEVELS + 1) * c, c), np.float32)
    for d in range(2):
        pi = np.arange(c) if d == 0 else c - 1 - np.arange(c)
        out[d, :c] = pi[None, :] <= pi[:, None]
        for lv in range(1, HG_MM_LEVELS + 1):
            m = 1 << lv
            mid = (pi // m) * m + m // 2
            late = (pi % m) >= m // 2
            sum_late = (pi[None, :] >= mid[:, None]) & (pi[None, :] <= pi[:, None])
            sum_early = (pi[None, :] > pi[:, None]) & (pi[None, :] <= mid[:, None] - 1)
            out[d, lv * c:(lv + 1) * c] = np.where(late[:, None], sum_late, sum_early)
    return out


def _hgrn_steps(q_ref, f_ref, i_ref, lb_ref, stack_ref, o_ref, st_ref, d):
    C = CHUNK
    H = HG_HEAD

    log_lb = lb_ref[d, 0:1, :]
    log1m_lb = lb_ref[d, 1:2, :]
    one_m_lb = lb_ref[d, 2:3, :]
    stack = stack_ref[d]

    row = lax.broadcasted_iota(jnp.int32, (TILE, HG_WIDTH), 0) & (C - 1)
    pi = row if d == 0 else C - 1 - row
    ii = lax.broadcasted_iota(jnp.int32, (C, C), 0)
    jj = lax.broadcasted_iota(jnp.int32, (C, C), 1)
    valid = {}
    for lv in range(1, LEVELS + 1):
        hi = (ii if d == 0 else C - 1 - ii) >> (lv - 1)
        hj = (jj if d == 0 else C - 1 - jj) >> (lv - 1)
        valid[lv] = (hi - hj + ((hj & 1) << 8)) == 1
    nchunk = TILE // C
    hsl = [slice(h * H, (h + 1) * H) for h in range(HG_HEADS)]
    ntdot = lambda a, b: lax.dot_general(a, b, (((1,), (1,)), ((), ())), preferred_element_type=F32)

    crows = [slice(c * C, (c + 1) * C) for c in range(nchunk)]
    cat0 = lambda xs: jnp.concatenate(xs, axis=0)
    xf = f_ref[0]
    nlog2e = -1.0 / float(np.log(2.0))
    lsig = jnp.minimum(xf, 0.0) - jnp.log(1.0 + jnp.exp2(jnp.abs(xf) * nlog2e))
    t2 = log1m_lb + lsig
    mx = jnp.maximum(log_lb, t2)
    logf = mx + jnp.log(1.0 + jnp.exp2(jnp.abs(log_lb - t2) * nlog2e))
    half_k = 0.5 * one_m_lb
    kin = half_k - half_k * jnp.tanh(0.5 * xf)
    qr = q_ref[0]
    half_s = 0.5 * HG_HEAD ** -0.5
    q = qr * (jnp.tanh(0.5 * qr) * half_s + half_s)
    vb = i_ref[0].astype(BF16)

    lf_terms = _split3(logf)[:2]
    e_chunk = [sum(jnp.dot(stack, t[rs], preferred_element_type=F32) for t in lf_terms)
               for rs in crows]
    cs = cat0([e[0:C] for e in e_chunk])
    tot = [jnp.sum(logf[rs], axis=0, keepdims=True) for rs in crows]
    qb = q.astype(BF16)
    kb = kin.astype(BF16)
    scores = {(c, h): jnp.where(ii == jj, ntdot(qb[rs, ls], kb[rs, ls]), 0.0)
              for c, rs in enumerate(crows) for h, ls in enumerate(hsl)}
    yield
    for lv in range(1, LEVELS + 1):
        m = 1 << lv
        late = ((pi >> (lv - 1)) & 1) == 1
        if lv <= HG_MM_LEVELS:
            e = cat0([ec[lv * C:(lv + 1) * C] for ec in e_chunk])
        else:
            at_row = [b * m + m // 2 - (1 if d == 0 else 0) for b in range(TILE // m)]
            cmid = cat0([jnp.broadcast_to(cs[r:r + 1], (m, HG_WIDTH)) for r in at_row])
            e = jnp.where(late, cs - cmid, cmid - cs)
        qk = (jnp.where(late, q, kin) * jnp.exp(e)).astype(BF16)
        scores = {(c, h): jnp.where(valid[lv], ntdot(qk[crows[c], hsl[h]], qk[crows[c], hsl[h]]), sc)
                  for (c, h), sc in scores.items()}
        yield
    tot_rows = cat0([jnp.broadcast_to(t, (C, HG_WIDTH)) for t in tot])
    q_in = (q * jnp.exp(cs)).astype(BF16)
    k_end = (kin * jnp.exp(tot_rows - cs)).astype(BF16)
    o_loc = {(c, h): jnp.dot(scores[c, h].astype(BF16), vb[crows[c], hsl[h]],
                             preferred_element_type=F32) for c, h in scores}
    kv = {(c, h): lax.dot_general(vb[crows[c], hsl[h]], k_end[crows[c], hsl[h]],
                                  (((0,), (0,)), ((), ())), preferred_element_type=F32)
          for c, h in scores}
    yield
    st = [st_ref[d, h] for h in range(HG_HEADS)]
    for step in range(nchunk):
        c = step if d == 0 else nchunk - 1 - step
        dec_end = jnp.exp(tot[c])
        for h, ls in enumerate(hsl):
            o_ref[0, crows[c], ls] = (ntdot(q_in[crows[c], ls], st[h].astype(BF16))
                                      + o_loc[c, h]).astype(o_ref.dtype)
        st = [st[h] * dec_end[:, ls] + kv[c, h] for h, ls in enumerate(hsl)]
        yield
    for h in range(HG_HEADS):
        st_ref[d, h] = st[h]


def _rw_tri():
    out = np.zeros((2, TILE, TILE), np.float32)
    idx = np.arange(TILE)
    same = (idx[:, None] // CHUNK) == (idx[None, :] // CHUNK)
    out[0] = same & (idx[None, :] <= idx[:, None])
    out[1] = same & (idx[None, :] >= idx[:, None])
    return out


def _head_ones():
    idx = np.arange(RW_WIDTH)
    return ((idx[:, None] // RW_HEAD) == (idx[None, :] // RW_HEAD)).astype(np.float32)


def _rwkv_pre_kernel(cur_ref, up_ref, dn_ref, mu_ref, vec_ref, dvec_ref, w2_ref, a2_ref, g2_ref,
                     hones_ref, tri_ref, rhat_ref, yloc_ref, mm_ref, nn_ref, bon_ref, g_ref,
                     *, nt, nctx):
    t = pl.program_id(1)
    C = CHUNK
    W = RW_WIDTH

    p = cur_ref[0]
    is_ctx = t < nctx
    rowi = lax.broadcasted_iota(jnp.int32, p.shape, 0)
    lane = lax.broadcasted_iota(jnp.int32, p.shape, 1)
    prev1 = pltpu.roll(p, 1, 0)
    next1 = pltpu.roll(p, TILE - 1, 0)
    seg = jnp.where(is_ctx, TILE - 1, GRID_W - 1)
    prev1 = jnp.where((rowi & seg) == 0, 0.0, prev1)
    next1 = jnp.where((rowi & seg) == seg, 0.0, next1)
    up_ok = t > nctx
    dn_ok = jnp.logical_and(t >= nctx, t < nt - 1)
    up = jnp.concatenate([jnp.where(up_ok, up_ref[0], 0.0), p[:TILE - GRID_W]], axis=0)
    dn = jnp.concatenate([p[GRID_W:], jnp.where(dn_ok, dn_ref[0], 0.0)], axis=0)
    cls = lane & 3
    sh_even = jnp.where(cls == 0, prev1, jnp.where(is_ctx, prev1, up))
    sh_odd = jnp.where(cls == 1, next1, jnp.where(is_ctx, next1, dn))
    shifted = jnp.where((cls & 1) == 0, sh_even, sh_odd)
    pf = p + mu_ref[...] * (shifted - p)

    r = pf[:, 0:W]
    k = pf[:, W:2 * W]
    v = pf[:, 2 * W:3 * W]
    o3 = 3 * W
    wd = pf[:, o3:o3 + 2 * RW_LORA]
    ad = pf[:, o3 + 2 * RW_LORA:o3 + 4 * RW_LORA]
    gd = pf[:, o3 + 4 * RW_LORA:o3 + 4 * RW_LORA + RW_GATE_LORA]

    k_k = vec_ref[0:1, :]
    half_ka = 0.5 * vec_ref[1:2, :]
    r_k = vec_ref[2:3, :]
    hones = hones_ref[...]

    kkr = k * k_k
    ss = _dot_exact_rhs(kkr * kkr, hones)
    kk = kkr * lax.rsqrt(jnp.maximum(ss, 1e-24))
    g_ref[0] = _bdot(_sigmoid(gd), g2_ref[...]).astype(g_ref.dtype)
    vb = v.astype(BF16)
    twd = jnp.tanh(wd).astype(BF16)
    adb = ad.astype(BF16)
    ksum = None
    dirs = (0, 1)
    nchunk = TILE // C
    atd, rtd, btd, ktd, gamd = {}, {}, {}, {}, {}
    for dd in dirs:
        lw = dvec_ref[dd, 0:1, :] + jnp.dot(twd, w2_ref[dd], preferred_element_type=F32)
        half_ld = -0.5 * float(np.exp(-0.5))
        ld = half_ld * jnp.tanh(0.5 * lw) + half_ld
        ta = jnp.tanh(0.5 * (dvec_ref[dd, 1:2, :]
                             + jnp.dot(adb, a2_ref[dd], preferred_element_type=F32)))
        a = 0.5 * ta + 0.5
        kdir = k * (1.0 + (ta - 1.0) * half_ka)
        ksum = kdir if ksum is None else ksum + kdir
        cs = _dot_exact_lhs(tri_ref[dd], ld)
        einv = jnp.exp(-cs)
        atd[dd] = (-kk * jnp.exp(cs - ld)).astype(BF16)
        rtd[dd] = (r * jnp.exp(cs)).astype(BF16)
        btd[dd] = (kk * a * einv).astype(BF16)
        ktd[dd] = (kdir * einv).astype(BF16)
        gamd[dd] = [jnp.exp(jnp.sum(ld[c * C:(c + 1) * C], axis=0, keepdims=True))
                    for c in range(nchunk)]
    bon_ref[0] = (_dot_exact_rhs(r * ksum * r_k, hones) * v).astype(bon_ref.dtype)

    pi_ = lax.broadcasted_iota(jnp.int32, (PAIR, PAIR), 0)
    pj_ = lax.broadcasted_iota(jnp.int32, (PAIR, PAIR), 1)
    same_head = (pi_ < RW_HEAD) == (pj_ < RW_HEAD)
    eye_pair = (pi_ == pj_).astype(F32)
    head0 = lax.broadcasted_iota(jnp.int32, (C, PAIR), 1) < RW_HEAD

    ii4 = lax.broadcasted_iota(jnp.int32, (C, 2 * PAIR), 0)
    jj4 = lax.broadcasted_iota(jnp.int32, (C, 2 * PAIR), 1) & (C - 1)
    before4 = [jj4 < ii4, jj4 > ii4]
    upto4 = [jj4 <= ii4, jj4 >= ii4]
    eye2 = (lax.broadcasted_iota(jnp.int32, (C, PAIR), 0)
            == (lax.broadcasted_iota(jnp.int32, (C, PAIR), 1) & (C - 1))).astype(F32)
    npair = RW_HEADS // 2
    lsl = [slice(pr * PAIR, (pr + 1) * PAIR) for pr in range(npair)]
    pairs = [(z, c, pr) for z in dirs for c in range(nchunk) for pr in range(npair)]
    fdot = lambda a, b: jnp.dot(a, b, preferred_element_type=F32)
    cat0 = lambda xs: jnp.concatenate(xs, axis=0)

    def bd(x):
        x = x.astype(F32)
        return cat0([jnp.where(head0, x, 0.0), jnp.where(head0, 0.0, x)]).astype(BF16)

    rows = lambda c: slice(c * C, (c + 1) * C)
    at = {(z, c, pr): atd[z][rows(c), lsl[pr]] for z, c, pr in pairs}
    rt = {(z, c, pr): rtd[z][rows(c), lsl[pr]] for z, c, pr in pairs}
    bt = {(z, c, pr): btd[z][rows(c), lsl[pr]] for z, c, pr in pairs}
    kt = {(z, c, pr): ktd[z][rows(c), lsl[pr]] for z, c, pr in pairs}
    vv = {(z, c, pr): vb[rows(c), lsl[pr]] for z, c, pr in pairs}
    gam = {(z, c, pr): gamd[z][c][:, lsl[pr]] for z, c, pr in pairs}
    bk = {pq: cat0([bt[pq], kt[pq]]) for pq in pairs}
    bdv = {pq: bd(vv[pq]) for pq in pairs}
    top, bot = {}, {}
    for pq in pairs:
        lhs = cat0([at[pq], rt[pq]])
        x = _bdot_nt(lhs, cat0([bd(bt[pq]), bd(kt[pq])]))
        top[pq] = jnp.where(before4[pq[0]], x[:C], 0.0)
        bot[pq] = jnp.where(upto4[pq[0]], x[C:], 0.0).astype(BF16)
    nf = {pq: top[pq][:, :PAIR] for pq in pairs}
    tinv = {pq: eye2 + nf[pq] for pq in pairs}
    pw = {pq: fdot(nf[pq].astype(BF16), bd(nf[pq])) for pq in pairs}
    w = {pq: fdot(top[pq][:, PAIR:].astype(BF16), bdv[pq]) for pq in pairs}
    for _ in range(LEVELS - 2):
        res = {pq: fdot(cat0([tinv[pq], pw[pq]]).astype(BF16), bd(pw[pq])) for pq in pairs}
        tinv = {pq: tinv[pq] + res[pq][:C] for pq in pairs}
        pw = {pq: res[pq][C:] for pq in pairs}
    tinv = {pq: tinv[pq] + fdot(tinv[pq].astype(BF16), bd(pw[pq])) for pq in pairs}
    tb = {pq: tinv[pq].astype(BF16) for pq in pairs}
    au = {pq: fdot(tb[pq], jnp.concatenate([bd(at[pq]), bd(w[pq])], axis=1)) for pq in pairs}
    ahf = {pq: au[pq][:, :PAIR] for pq in pairs}
    uvf = {pq: au[pq][:, PAIR:] for pq in pairs}
    for z, c, pr in pairs:
        pq = (z, c, pr)
        rhat_ref[z, 0, rows(c), lsl[pr]] = (
            rt[pq] + fdot(bot[pq][:, :PAIR], bd(ahf[pq]))).astype(rhat_ref.dtype)
        yloc_ref[z, 0, rows(c), lsl[pr]] = fdot(
            bot[pq], cat0([bd(uvf[pq]), bdv[pq]])).astype(yloc_ref.dtype)
    for z, c, pr in pairs:
        pq = (z, c, pr)
        mm_ref[z, 0, c * npair + pr] = (
            jnp.where(same_head, eye_pair + _bdot_tn(ahf[pq], bk[pq][:C]), 0.0) * gam[pq]
        ).astype(mm_ref.dtype)
        nn_ref[z, 0, c * npair + pr] = jnp.where(
            same_head, _bdot_tn(cat0([uvf[pq].astype(BF16), vv[pq]]), bk[pq]), 0.0) * gam[pq]


def _mixer_kernel(lb_ref, stack_ref, *refs):
    nin = 7
    per_dir = [refs[z * nin:(z + 1) * nin] for z in range(2)]
    o_refs = [refs[2 * nin], refs[2 * nin + 2]]
    y_refs = [refs[2 * nin + 1], refs[2 * nin + 3]]
    st_ref, s_ref = refs[2 * nin + 4:]
    s = pl.program_id(1)
    C = CHUNK
    nchunk = TILE // C
    npair = RW_HEADS // 2
    dirs = (0, 1)
    lsl = [slice(pr * PAIR, (pr + 1) * PAIR) for pr in range(npair)]

    @pl.when(s == 0)
    def _():
        st_ref[...] = jnp.zeros_like(st_ref)
        s_ref[...] = jnp.zeros_like(s_ref)

    hgrn = [_hgrn_steps(*per_dir[z][:3], lb_ref, stack_ref, o_refs[z], st_ref, z) for z in dirs]

    def fill(n):
        for _ in range(n):
            for gen in hgrn:
                next(gen, None)

    rhat_ref, yloc_ref, mm_ref, nn_ref = zip(*[per_dir[z][3:] for z in dirs])
    st = {(z, pr): s_ref[z, pr] for z in dirs for pr in range(npair)}
    for u in range(nchunk):
        fill(HG_FILL)
        stb = {k: x.astype(BF16) for k, x in st.items()}
        for z in dirs:
            c = u if z == 0 else nchunk - 1 - u
            for pr in range(npair):
                y_refs[z][0, c * C:(c + 1) * C, lsl[pr]] = (
                    _bdot_nt(rhat_ref[z][0, 0, c * C:(c + 1) * C, lsl[pr]], stb[z, pr])
                    + yloc_ref[z][0, 0, c * C:(c + 1) * C, lsl[pr]].astype(F32)
                ).astype(y_refs[z].dtype)
            for pr in range(npair):
                st[z, pr] = (_bdot(stb[z, pr], mm_ref[z][0, 0, c * npair + pr])
                             + nn_ref[z][0, 0, c * npair + pr])
    for (z, pr), x in st.items():
        s_ref[z, pr] = x
    for gen in hgrn:
        for _ in gen:
            pass


def _rwkv_pre(p_rw, mu, vecs, dvecs, w2cat, a2cat, g2, hones, tri, nctx):
    B, TC, _ = p_rw.shape
    nt = TC // TILE
    hb = TILE // GRID_W
    nhb = TC // GRID_W
    ntr = (TILE // CHUNK) * (RW_HEADS // 2)
    full = lambda a: pl.BlockSpec(a.shape, lambda b, t: (0,) * a.ndim)
    both = pl.BlockSpec((2, 1, TILE, RW_WIDTH), lambda b, t: (0, b, t, 0))
    trans = pl.BlockSpec((2, 1, ntr, PAIR, PAIR), lambda b, t: (0, b, t, 0, 0))
    one = pl.BlockSpec((1, TILE, RW_WIDTH), lambda b, t: (b, t, 0))
    dir_bf = jax.ShapeDtypeStruct((2, B, TC, RW_WIDTH), BF16)
    return pl.pallas_call(
        functools.partial(_rwkv_pre_kernel, nt=nt, nctx=nctx),
        grid=(B, nt),
        in_specs=[pl.BlockSpec((1, TILE, RW_COLS), lambda b, t: (b, t, 0)),
                  pl.BlockSpec((1, GRID_W, RW_COLS), lambda b, t: (b, jnp.maximum(t * hb - 1, 0), 0)),
                  pl.BlockSpec((1, GRID_W, RW_COLS),
                               lambda b, t: (b, jnp.minimum((t + 1) * hb, nhb - 1), 0)),
                  full(mu), full(vecs), full(dvecs), full(w2cat), full(a2cat), full(g2),
                  full(hones), full(tri)],
        out_specs=[both, both, trans, trans, one, one],
        out_shape=[dir_bf, dir_bf,
                   jax.ShapeDtypeStruct((2, B, nt * ntr, PAIR, PAIR), BF16),
                   jax.ShapeDtypeStruct((2, B, nt * ntr, PAIR, PAIR), F32),
                   jax.ShapeDtypeStruct((B, TC, RW_WIDTH), BF16),
                   jax.ShapeDtypeStruct((B, TC, RW_WIDTH), BF16)],
        compiler_params=_params("arbitrary", "arbitrary"),
        name="rwkv7_pre",
    )(p_rw, p_rw, p_rw, mu, vecs, dvecs, w2cat, a2cat, g2, hones, tri)


def _mixer(p_hg, lbtab, stack, rhat, yloc, mm, nn, nctx):
    _, B, TC, _ = rhat.shape
    nt = TC // TILE
    ntr = (TILE // CHUNK) * (RW_HEADS // 2)
    assert HG_WIDTH == RW_WIDTH
    full = lambda a: pl.BlockSpec(a.shape, lambda b, s: (0,) * a.ndim)
    in_specs, operands = [full(lbtab), full(stack)], [lbtab, stack]
    out_specs = []
    for z in range(2):
        tile = functools.partial(_scan_tile, z, nt=nt, nctx=nctx)
        col = lambda c, tile=tile: pl.BlockSpec((1, TILE, HG_WIDTH), lambda b, s: (b, tile(s), c))
        dir_tile = pl.BlockSpec((1, 1, TILE, RW_WIDTH), lambda b, s, z=z, tile=tile: (z, b, tile(s), 0))
        trans = pl.BlockSpec((1, 1, ntr, PAIR, PAIR), lambda b, s, z=z, tile=tile: (z, b, tile(s), 0, 0))
        in_specs += [col(0), col(1 + z), col(3), dir_tile, dir_tile, trans, trans]
        operands += [p_hg, p_hg, p_hg, rhat, yloc, mm, nn]
        out_specs += [col(0), col(0)]
    out_shape = jax.ShapeDtypeStruct((B, TC, RW_WIDTH), BF16)
    return pl.pallas_call(
        _mixer_kernel,
        grid=(B, nt),
        in_specs=in_specs,
        out_specs=out_specs,
        out_shape=[out_shape] * 4,
        scratch_shapes=[pltpu.VMEM((2, HG_HEADS, HG_HEAD, HG_HEAD), F32),
                        pltpu.VMEM((2, RW_HEADS // 2, PAIR, PAIR), F32)],
        compiler_params=_params("arbitrary", "arbitrary"),
        name="mixer_scans",
    )(*operands)


def _merge_kernel(of_ref, ob_ref, og_ref, yf_ref, yb_ref, bon_ref, g_ref, gt_ref, x_ref, mod_ref,
                  hgn_ref, gn_ref, ng_ref, hones_ref, pa_ref, pb_ref, wo_ref, o_ref):
    D = x_ref.shape[-1]
    H = HG_HEAD
    oh = of_ref[0].astype(F32) + ob_ref[0].astype(F32)
    og = og_ref[0]
    hgn = hgn_ref[...]
    parts = []
    for h in range(HG_HEADS):
        ls = slice(h * H, (h + 1) * H)
        parts.append(_rms(oh[:, ls], hgn) * (og[:, ls] * _sigmoid(og[:, ls])))
    o_hg = jnp.concatenate(parts, axis=-1)
    hones = hones_ref[...]
    y = yf_ref[0].astype(F32) + yb_ref[0].astype(F32)
    mean = _dot_exact_rhs(y, hones) * (1.0 / RW_HEAD)
    yc = y - mean
    var = _dot_exact_rhs(yc * yc, hones) * (1.0 / RW_HEAD)
    yn = yc * lax.rsqrt(var + RW_GN_EPS) * gn_ref[0:1, :] + gn_ref[1:2, :]
    o_rw = (yn + bon_ref[0].astype(F32)) * g_ref[0].astype(F32)
    gt = gt_ref[0].astype(F32)
    m = (_sigmoid(gt[:, :D]) * _bdot(o_hg, pa_ref[...])
         + _sigmoid(gt[:, D:]) * _bdot(o_rw, pb_ref[...]))
    yx = _bdot(m, wo_ref[...])
    gate = mod_ref[0, :, 2 * D:3 * D]
    o_ref[0] = x_ref[0] + gate * _rms(yx, ng_ref[...])


def _merge(o_f, o_b, p_hg, y_f, y_b, bon, g, p_gt, xcat, modtab, hgn, gn, ng, hones, pa, pb, wo, t0):
    B, TC, D = xcat.shape
    nt = TC // TILE - t0
    full = lambda a: pl.BlockSpec(a.shape, lambda b, t: (0,) * a.ndim)
    tok = pl.BlockSpec((1, TILE, RW_WIDTH), lambda b, t: (b, t + t0, 0))
    return pl.pallas_call(
        _merge_kernel,
        grid=(B, nt),
        in_specs=[tok, tok,
                  pl.BlockSpec((1, TILE, HG_WIDTH), lambda b, t: (b, t + t0, 4)),
                  tok, tok, tok, tok,
                  pl.BlockSpec((1, TILE, 2 * D), lambda b, t: (b, t + t0, 0)),
                  pl.BlockSpec((1, TILE, D), lambda b, t: (b, t + t0, 0)),
                  pl.BlockSpec((1, 1, modtab.shape[-1]),
                               lambda b, t: (2 * b + jnp.minimum(t + t0, 1), 0, 0)),
                  full(hgn), full(gn), full(ng), full(hones), full(pa), full(pb), full(wo)],
        out_specs=pl.BlockSpec((1, TILE, D), lambda b, t: (b, t, 0)),
        out_shape=jax.ShapeDtypeStruct((B, nt * TILE, D), F32),
        compiler_params=_params("arbitrary", "arbitrary"),
        name="gated_merge",
    )(o_f, o_b, p_hg, y_f, y_b, bon, g, p_gt, xcat, modtab, hgn, gn, ng, hones, pa, pb, wo)


def _ffn_kernel(x_ref, mod_ref, g_ref, w1_ref, w2_ref, o_ref):
    D = x_ref.shape[-1]
    F = w2_ref.shape[0]
    x = x_ref[0]
    h = _rms(x, g_ref[0:1, :]) * (1.0 + mod_ref[0, :, 4 * D:5 * D]) + mod_ref[0, :, 3 * D:4 * D]
    hb = h.astype(BF16)
    gt = jnp.dot(hb, w1_ref[:, 0:F], preferred_element_type=F32)
    up = jnp.dot(hb, w1_ref[:, F:2 * F], preferred_element_type=F32)
    act = (gt * _sigmoid(gt) * up).astype(BF16)
    y = jnp.dot(act, w2_ref[...], preferred_element_type=F32)
    o_ref[0] = x + mod_ref[0, :, 5 * D:6 * D] * _rms(y, g_ref[1:2, :])


def _ffn(x, modtab, g, w1, w2, t0):
    B, TX, D = x.shape
    nt = TX // TILE
    full = lambda a: pl.BlockSpec(a.shape, lambda b, t: (0,) * a.ndim)
    return pl.pallas_call(
        _ffn_kernel,
        grid=(B, nt),
        in_specs=[pl.BlockSpec((1, TILE, D), lambda b, t: (b, t, 0)),
                  pl.BlockSpec((1, 1, modtab.shape[-1]),
                               lambda b, t: (2 * b + jnp.minimum(t + t0, 1), 0, 0)),
                  full(g), full(w1), full(w2)],
        out_specs=pl.BlockSpec((1, TILE, D), lambda b, t: (b, t, 0)),
        out_shape=jax.ShapeDtypeStruct((B, TX, D), F32),
        compiler_params=_params("arbitrary", "arbitrary"),
        name="swiglu_ffn",
    )(x, modtab, g, w1, w2)


def kernel(x, c, ctx, c_ctx, ada_w, ada_b, norm_g, w_in, hg_lb_logits, hg_norm_g, rw_mu, rw_w0, rw_w2,
           rw_a0, rw_a2, rw_g2, rw_kk, rw_ka, rw_rk, rw_gn_g, rw_gn_b, proj_a, proj_b, w_out, ffn_w1,
           ffn_w2):
    B, T, D = x.shape
    LCTX = ctx.shape[1]
    L = ada_w.shape[0]
    assert T % TILE == 0 and LCTX == TILE and TILE % GRID_W == 0 and T // TILE >= 2
    nctx = LCTX // TILE

    rows = -(-(B + 1) // 8) * 8
    cond = jnp.zeros((rows, D), F32).at[:B].set(c).at[B].set(c_ctx)
    mods = _mods(cond, ada_w, ada_b)

    lb_cum = jnp.cumsum(jax.nn.softmax(hg_lb_logits.astype(F32), axis=0), axis=0)
    hg_lb = lb_cum - lb_cum[:1]
    lbtab = jnp.stack([jnp.log(hg_lb), jnp.log1p(-hg_lb), 1.0 - hg_lb], axis=2)

    stack = jnp.asarray(_hgrn_stack(), BF16)
    tri = jnp.asarray(_rw_tri(), BF16)
    hones = jnp.asarray(_head_ones(), BF16)

    zeros = jnp.zeros_like(rw_w2[:, 0])
    w2cat = jnp.stack([jnp.concatenate([rw_w2[:, 0], zeros], axis=1),
                       jnp.concatenate([zeros, rw_w2[:, 1]], axis=1)], axis=1).astype(BF16)
    a2cat = jnp.stack([jnp.concatenate([rw_a2[:, 0], zeros], axis=1),
                       jnp.concatenate([zeros, rw_a2[:, 1]], axis=1)], axis=1).astype(BF16)

    xcat = jnp.concatenate([ctx, x], axis=1)
    for l in range(L):
        last = l == L - 1
        lat = mods[l, :B]
        ctxm = jnp.broadcast_to(mods[l, B], lat.shape)
        modtab = jnp.stack([ctxm, lat], axis=1).reshape(2 * B, 1, 6 * D)
        wl = w_in[l].astype(BF16)
        p_hg, p_rw, p_gt = _inproj(xcat, modtab, norm_g[l, 0:1], wl[:, :HG_COLS],
                                   wl[:, HG_COLS:HG_COLS + RW_COLS], wl[:, HG_COLS + RW_COLS:])
        vecs = jnp.stack([rw_kk[l], rw_ka[l], rw_rk[l].reshape(-1)], axis=0)
        dvecs = jnp.stack([rw_w0[l], rw_a0[l]], axis=1)
        rhat, yloc, mm, nn, bon, g = _rwkv_pre(p_rw, rw_mu[l][None, :], vecs, dvecs, w2cat[l],
                                               a2cat[l], rw_g2[l].astype(BF16), hones, tri, nctx)
        o_f, y_f, o_b, y_b = _mixer(p_hg, lbtab[l], stack, rhat, yloc, mm, nn, nctx)
        t0 = nctx if last else 0
        xm = _merge(o_f, o_b, p_hg, y_f, y_b, bon, g, p_gt, xcat, modtab, hg_norm_g[l][None, :],
                    jnp.stack([rw_gn_g[l], rw_gn_b[l]], axis=0), norm_g[l, 1:2], hones,
                    proj_a[l].astype(BF16), proj_b[l].astype(BF16), w_out[l].astype(BF16), t0)
        xcat = _ffn(xm, modtab, norm_g[l, 2:4], ffn_w1[l].astype(BF16), ffn_w2[l].astype(BF16), t0)
    return xcat
```

```python
import functools

import numpy as np
import jax
import jax.numpy as jnp
from jax import lax
from jax.experimental import pallas as pl
from jax.experimental.pallas import tpu as pltpu

F32 = jnp.float32
BF16 = jnp.bfloat16

NORM_EPS = 1e-6
GRID_W = 64
HG_HEADS = 4
HG_HEAD = 128
HG_WIDTH = HG_HEADS * HG_HEAD
RW_HEADS = 8
RW_HEAD = 64
RW_WIDTH = RW_HEADS * RW_HEAD
RW_LORA = 64
RW_GATE_LORA = 128
RW_GN_EPS = 64e-5
RW_COLS = 3 * RW_WIDTH + 4 * RW_LORA + RW_GATE_LORA
HG_COLS = 5 * HG_WIDTH

TILE = 256
CHUNK = 64
LEVELS = 6
HG_MM_LEVELS = 3
HG_FILL = 3
PAIR = 2 * RW_HEAD
VMEM_LIMIT = 56 * 1024 * 1024


def _bdot(a, b):
    return jnp.dot(a.astype(BF16), b.astype(BF16), preferred_element_type=F32)


def _bdot_nt(a, b):
    return lax.dot_general(a.astype(BF16), b.astype(BF16), (((1,), (1,)), ((), ())),
                           preferred_element_type=F32)


def _bdot_tn(a, b):
    return lax.dot_general(a.astype(BF16), b.astype(BF16), (((0,), (0,)), ((), ())),
                           preferred_element_type=F32)


def _split3(x):
    hi = x.astype(BF16)
    r1 = x - hi.astype(F32)
    mid = r1.astype(BF16)
    lo = (r1 - mid.astype(F32)).astype(BF16)
    return hi, mid, lo


def _dot_exact_lhs(m01, x, terms=2):
    d = lambda t: jnp.dot(m01, t, preferred_element_type=F32)
    return sum(d(t) for t in _split3(x)[:terms])


def _dot_exact_rhs(x, m01, terms=2):
    d = lambda t: jnp.dot(t, m01, preferred_element_type=F32)
    return sum(d(t) for t in _split3(x)[:terms])


def _rms(x, g):
    ms = jnp.mean(x * x, axis=-1, keepdims=True)
    return x * lax.rsqrt(ms + NORM_EPS) * g


def _sigmoid(x):
    return 0.5 * jnp.tanh(0.5 * x) + 0.5


def _params(*sem):
    return pltpu.CompilerParams(dimension_semantics=sem, vmem_limit_bytes=VMEM_LIMIT)


def _mods_kernel(c_ref, w_ref, b_ref, o_ref):
    c = c_ref[...]
    s = c * _sigmoid(c)
    sh, sm, sl = _split3(s)
    wh, wm, wl = _split3(w_ref[0])
    d = lambda a, b: jnp.dot(a, b, preferred_element_type=F32)
    acc = d(sh, wh) + d(sh, wm) + d(sm, wh) + d(sh, wl) + d(sl, wh) + d(sm, wm)
    o_ref[0] = acc + b_ref[0]


def _mods(cond, ada_w, ada_b):
    L, D, N = ada_w.shape
    rows = cond.shape[0]
    tn = 1536
    return pl.pallas_call(
        _mods_kernel,
        grid=(L, N // tn),
        in_specs=[pl.BlockSpec((rows, D), lambda l, n: (0, 0)),
                  pl.BlockSpec((1, D, tn), lambda l, n: (l, 0, n)),
                  pl.BlockSpec((1, 1, tn), lambda l, n: (l, 0, n))],
        out_specs=pl.BlockSpec((1, rows, tn), lambda l, n: (l, 0, n)),
        out_shape=jax.ShapeDtypeStruct((L, rows, N), F32),
        compiler_params=_params("arbitrary", "arbitrary"),
        name="adaln_mods",
    )(cond, ada_w, ada_b.reshape(L, 1, N))


def _inproj_kernel(x_ref, mod_ref, g_ref, whg_ref, wrw_ref, wgt_ref, ohg_ref, orw_ref, ogt_ref):
    D = x_ref.shape[-1]
    y = _rms(x_ref[0], g_ref[...])
    shift = mod_ref[0, :, 0:D]
    scale = mod_ref[0, :, D:2 * D]
    h = (y * (1.0 + scale) + shift).astype(BF16)
    ohg_ref[0] = jnp.dot(h, whg_ref[...], preferred_element_type=F32)
    orw_ref[0] = jnp.dot(h, wrw_ref[...], preferred_element_type=F32)
    ogt_ref[0] = jnp.dot(h, wgt_ref[...], preferred_element_type=F32).astype(ogt_ref.dtype)


def _inproj(xcat, modtab, g, whg, wrw, wgt):
    B, TC, D = xcat.shape
    nt = TC // TILE
    full = lambda a: pl.BlockSpec(a.shape, lambda b, t: (0,) * a.ndim)
    outs = [jax.ShapeDtypeStruct((B, TC, w.shape[1]), dt) for w, dt in ((whg, F32), (wrw, F32), (wgt, BF16))]
    return pl.pallas_call(
        _inproj_kernel,
        grid=(B, nt),
        in_specs=[pl.BlockSpec((1, TILE, D), lambda b, t: (b, t, 0)),
                  pl.BlockSpec((1, 1, modtab.shape[-1]), lambda b, t: (2 * b + jnp.minimum(t, 1), 0, 0)),
                  full(g), full(whg), full(wrw), full(wgt)],
        out_specs=[pl.BlockSpec((1, TILE, o.shape[-1]), lambda b, t: (b, t, 0)) for o in outs],
        out_shape=outs,
        compiler_params=_params("arbitrary", "arbitrary"),
        name="in_proj",
    )(xcat, modtab, g, whg, wrw, wgt)


def _scan_tile(d, s, nt, nctx):
    bwd = jnp.where(s < nctx, nctx - 1 - s, nt - 1 - (s - nctx))
    return jnp.where(d == 0, s, bwd)


def _hgrn_stack():
    c = CHUNK
    out = np.zeros((2, (HG_MM_LEVELS + 1) * c, c), np.float32)
    for d in range(2):
        pi = np.arange(c) if d == 0 else c - 1 - np.arange(c)
        out[d, :c] = pi[None, :] <= pi[:, None]
        for lv in range(1, HG_MM_LEVELS + 1):
            m = 1 << lv
            mid = (pi // m) * m + m // 2
            late = (pi % m) >= m // 2
            sum_late = (pi[None, :] >= mid[:, None]) & (pi[None, :] <= pi[:, None])
            sum_early = (pi[None, :] > pi[:, None]) & (pi[None, :] <= mid[:, None] - 1)
            out[d, lv * c:(lv + 1) * c] = np.where(late[:, None], sum_late, sum_early)
    return out


def _hgrn_steps(q_ref, f_ref, i_ref, lb_ref, stack_ref, o_ref, st_ref, d):
    C = CHUNK
    H = HG_HEAD

    log_lb = lb_ref[d, 0:1, :]
    log1m_lb = lb_ref[d, 1:2, :]
    one_m_lb = lb_ref[d, 2:3, :]
    stack = stack_ref[d]

    row = lax.broadcasted_iota(jnp.int32, (TILE, HG_WIDTH), 0) & (C - 1)
    pi = row if d == 0 else C - 1 - row
    ii = lax.broadcasted_iota(jnp.int32, (C, C), 0)
    jj = lax.broadcasted_iota(jnp.int32, (C, C), 1)
    valid = {}
    for lv in range(1, LEVELS + 1):
        hi = (ii if d == 0 else C - 1 - ii) >> (lv - 1)
        hj = (jj if d == 0 else C - 1 - jj) >> (lv - 1)
        valid[lv] = (hi - hj + ((hj & 1) << 8)) == 1
    nchunk = TILE // C
    hsl = [slice(h * H, (h + 1) * H) for h in range(HG_HEADS)]
    ntdot = lambda a, b: lax.dot_general(a, b, (((1,), (1,)), ((), ())), preferred_element_type=F32)

    crows = [slice(c * C, (c + 1) * C) for c in range(nchunk)]
    cat0 = lambda xs: jnp.concatenate(xs, axis=0)
    xf = f_ref[0]
    nlog2e = -1.0 / float(np.log(2.0))
    lsig = jnp.minimum(xf, 0.0) - jnp.log(1.0 + jnp.exp2(jnp.abs(xf) * nlog2e))
    t2 = log1m_lb + lsig
    mx = jnp.maximum(log_lb, t2)
    logf = mx + jnp.log(1.0 + jnp.exp2(jnp.abs(log_lb - t2) * nlog2e))
    half_k = 0.5 * one_m_lb
    kin = half_k - half_k * jnp.tanh(0.5 * xf)
    qr = q_ref[0]
    half_s = 0.5 * HG_HEAD ** -0.5
    q = qr * (jnp.tanh(0.5 * qr) * half_s + half_s)
    vb = i_ref[0].astype(BF16)

    lf_terms = _split3(logf)[:2]
    e_chunk = [sum(jnp.dot(stack, t[rs], preferred_element_type=F32) for t in lf_terms)
               for rs in crows]
    cs = cat0([e[0:C] for e in e_chunk])
    tot = [jnp.sum(logf[rs], axis=0, keepdims=True) for rs in crows]
    qb = q.astype(BF16)
    kb = kin.astype(BF16)
    scores = {(c, h): jnp.where(ii == jj, ntdot(qb[rs, ls], kb[rs, ls]), 0.0)
              for c, rs in enumerate(crows) for h, ls in enumerate(hsl)}
    yield
    for lv in range(1, LEVELS + 1):
        m = 1 << lv
        late = ((pi >> (lv - 1)) & 1) == 1
        if lv <= HG_MM_LEVELS:
            e = cat0([ec[lv * C:(lv + 1) * C] for ec in e_chunk])
        else:
            at_row = [b * m + m // 2 - (1 if d == 0 else 0) for b in range(TILE // m)]
            cmid = cat0([jnp.broadcast_to(cs[r:r + 1], (m, HG_WIDTH)) for r in at_row])
            e = jnp.where(late, cs - cmid, cmid - cs)
        qk = (jnp.where(late, q, kin) * jnp.exp(e)).astype(BF16)
        scores = {(c, h): jnp.where(valid[lv], ntdot(qk[crows[c], hsl[h]], qk[crows[c], hsl[h]]), sc)
                  for (c, h), sc in scores.items()}
        yield
    tot_rows = cat0([jnp.broadcast_to(t, (C, HG_WIDTH)) for t in tot])
    q_in = (q * jnp.exp(cs)).astype(BF16)
    k_end = (kin * jnp.exp(tot_rows - cs)).astype(BF16)
    o_loc = {(c, h): jnp.dot(scores[c, h].astype(BF16), vb[crows[c], hsl[h]],
                             preferred_element_type=F32) for c, h in scores}
    kv = {(c, h): lax.dot_general(vb[crows[c], hsl[h]], k_end[crows[c], hsl[h]],
                                  (((0,), (0,)), ((), ())), preferred_element_type=F32)
          for c, h in scores}
    yield
    st = [st_ref[d, h] for h in range(HG_HEADS)]
    for step in range(nchunk):
        c = step if d == 0 else nchunk - 1 - step
        dec_end = jnp.exp(tot[c])
        for h, ls in enumerate(hsl):
            o_ref[0, crows[c], ls] = (ntdot(q_in[crows[c], ls], st[h].astype(BF16))
                                      + o_loc[c, h]).astype(o_ref.dtype)
        st = [st[h] * dec_end[:, ls] + kv[c, h] for h, ls in enumerate(hsl)]
        yield
    for h in range(HG_HEADS):
        st_ref[d, h] = st[h]


def _rw_tri():
    out = np.zeros((2, TILE, TILE), np.float32)
    idx = np.arange(TILE)
    same = (idx[:, None] // CHUNK) == (idx[None, :] // CHUNK)
    out[0] = same & (idx[None, :] <= idx[:, None])
    out[1] = same & (idx[None, :] >= idx[:, None])
    return out


def _head_ones():
    idx = np.arange(RW_WIDTH)
    return ((idx[:, None] // RW_HEAD) == (idx[None, :] // RW_HEAD)).astype(np.float32)


def _rwkv_pre_kernel(cur_ref, up_ref, dn_ref, mu_ref, vec_ref, dvec_ref, w2_ref, a2_ref, g2_ref,
                     hones_ref, tri_ref, rhat_ref, yloc_ref, mm_ref, nn_ref, bon_ref, g_ref,
                     *, nt, nctx):
    t = pl.program_id(1)
    C = CHUNK
    W = RW_WIDTH

    p = cur_ref[0]
    is_ctx = t < nctx
    rowi = lax.broadcasted_iota(jnp.int32, p.shape, 0)
    lane = lax.broadcasted_iota(jnp.int32, p.shape, 1)
    prev1 = pltpu.roll(p, 1, 0)
    next1 = pltpu.roll(p, TILE - 1, 0)
    seg = jnp.where(is_ctx, TILE - 1, GRID_W - 1)
    prev1 = jnp.where((rowi & seg) == 0, 0.0, prev1)
    next1 = jnp.where((rowi & seg) == seg, 0.0, next1)
    up_ok = t > nctx
    dn_ok = jnp.logical_and(t >= nctx, t < nt - 1)
    up = jnp.concatenate([jnp.where(up_ok, up_ref[0], 0.0), p[:TILE - GRID_W]], axis=0)
    dn = jnp.concatenate([p[GRID_W:], jnp.where(dn_ok, dn_ref[0], 0.0)], axis=0)
    cls = lane & 3
    sh_even = jnp.where(cls == 0, prev1, jnp.where(is_ctx, prev1, up))
    sh_odd = jnp.where(cls == 1, next1, jnp.where(is_ctx, next1, dn))
    shifted = jnp.where((cls & 1) == 0, sh_even, sh_odd)
    pf = p + mu_ref[...] * (shifted - p)

    r = pf[:, 0:W]
    k = pf[:, W:2 * W]
    v = pf[:, 2 * W:3 * W]
    o3 = 3 * W
    wd = pf[:, o3:o3 + 2 * RW_LORA]
    ad = pf[:, o3 + 2 * RW_LORA:o3 + 4 * RW_LORA]
    gd = pf[:, o3 + 4 * RW_LORA:o3 + 4 * RW_LORA + RW_GATE_LORA]

    k_k = vec_ref[0:1, :]
    half_ka = 0.5 * vec_ref[1:2, :]
    r_k = vec_ref[2:3, :]
    hones = hones_ref[...]

    kkr = k * k_k
    ss = _dot_exact_rhs(kkr * kkr, hones)
    kk = kkr * lax.rsqrt(jnp.maximum(ss, 1e-24))
    g_ref[0] = _bdot(_sigmoid(gd), g2_ref[...]).astype(g_ref.dtype)
    vb = v.astype(BF16)
    twd = jnp.tanh(wd).astype(BF16)
    adb = ad.astype(BF16)
    ksum = None
    dirs = (0, 1)
    nchunk = TILE // C
    atd, rtd, btd, ktd, gamd = {}, {}, {}, {}, {}
    for dd in dirs:
        lw = dvec_ref[dd, 0:1, :] + jnp.dot(twd, w2_ref[dd], preferred_element_type=F32)
        half_ld = -0.5 * float(np.exp(-0.5))
        ld = half_ld * jnp.tanh(0.5 * lw) + half_ld
        ta = jnp.tanh(0.5 * (dvec_ref[dd, 1:2, :]
                             + jnp.dot(adb, a2_ref[dd], preferred_element_type=F32)))
        a = 0.5 * ta + 0.5
        kdir = k * (1.0 + (ta - 1.0) * half_ka)
        ksum = kdir if ksum is None else ksum + kdir
        cs = _dot_exact_lhs(tri_ref[dd], ld)
        einv = jnp.exp(-cs)
        atd[dd] = (-kk * jnp.exp(cs - ld)).astype(BF16)
        rtd[dd] = (r * jnp.exp(cs)).astype(BF16)
        btd[dd] = (kk * a * einv).astype(BF16)
        ktd[dd] = (kdir * einv).astype(BF16)
        gamd[dd] = [jnp.exp(jnp.sum(ld[c * C:(c + 1) * C], axis=0, keepdims=True))
                    for c in range(nchunk)]
    bon_ref[0] = (_dot_exact_rhs(r * ksum * r_k, hones) * v).astype(bon_ref.dtype)

    pi_ = lax.broadcasted_iota(jnp.int32, (PAIR, PAIR), 0)
    pj_ = lax.broadcasted_iota(jnp.int32, (PAIR, PAIR), 1)
    same_head = (pi_ < RW_HEAD) == (pj_ < RW_HEAD)
    eye_pair = (pi_ == pj_).astype(F32)
    head0 = lax.broadcasted_iota(jnp.int32, (C, PAIR), 1) < RW_HEAD

    ii4 = lax.broadcasted_iota(jnp.int32, (C, 2 * PAIR), 0)
    jj4 = lax.broadcasted_iota(jnp.int32, (C, 2 * PAIR), 1) & (C - 1)
    before4 = [jj4 < ii4, jj4 > ii4]
    upto4 = [jj4 <= ii4, jj4 >= ii4]
    eye2 = (lax.broadcasted_iota(jnp.int32, (C, PAIR), 0)
            == (lax.broadcasted_iota(jnp.int32, (C, PAIR), 1) & (C - 1))).astype(F32)
    npair = RW_HEADS // 2
    lsl = [slice(pr * PAIR, (pr + 1) * PAIR) for pr in range(npair)]
    pairs = [(z, c, pr) for z in dirs for c in range(nchunk) for pr in range(npair)]
    fdot = lambda a, b: jnp.dot(a, b, preferred_element_type=F32)
    cat0 = lambda xs: jnp.concatenate(xs, axis=0)

    def bd(x):
        x = x.astype(F32)
        return cat0([jnp.where(head0, x, 0.0), jnp.where(head0, 0.0, x)]).astype(BF16)

    rows = lambda c: slice(c * C, (c + 1) * C)
    at = {(z, c, pr): atd[z][rows(c), lsl[pr]] for z, c, pr in pairs}
    rt = {(z, c, pr): rtd[z][rows(c), lsl[pr]] for z, c, pr in pairs}
    bt = {(z, c, pr): btd[z][rows(c), lsl[pr]] for z, c, pr in pairs}
    kt = {(z, c, pr): ktd[z][rows(c), lsl[pr]] for z, c, pr in pairs}
    vv = {(z, c, pr): vb[rows(c), lsl[pr]] for z, c, pr in pairs}
    gam = {(z, c, pr): gamd[z][c][:, lsl[pr]] for z, c, pr in pairs}
    bk = {pq: cat0([bt[pq], kt[pq]]) for pq in pairs}
    bdv = {pq: bd(vv[pq]) for pq in pairs}
    top, bot = {}, {}
    for pq in pairs:
        lhs = cat0([at[pq], rt[pq]])
        x = _bdot_nt(lhs, cat0([bd(bt[pq]), bd(kt[pq])]))
        top[pq] = jnp.where(before4[pq[0]], x[:C], 0.0)
        bot[pq] = jnp.where(upto4[pq[0]], x[C:], 0.0).astype(BF16)
    nf = {pq: top[pq][:, :PAIR] for pq in pairs}
    tinv = {pq: eye2 + nf[pq] for pq in pairs}
    pw = {pq: fdot(nf[pq].astype(BF16), bd(nf[pq])) for pq in pairs}
    w = {pq: fdot(top[pq][:, PAIR:].astype(BF16), bdv[pq]) for pq in pairs}
    for _ in range(LEVELS - 2):
        res = {pq: fdot(cat0([tinv[pq], pw[pq]]).astype(BF16), bd(pw[pq])) for pq in pairs}
        tinv = {pq: tinv[pq] + res[pq][:C] for pq in pairs}
        pw = {pq: res[pq][C:] for pq in pairs}
    tinv = {pq: tinv[pq] + fdot(tinv[pq].astype(BF16), bd(pw[pq])) for pq in pairs}
    tb = {pq: tinv[pq].astype(BF16) for pq in pairs}
    au = {pq: fdot(tb[pq], jnp.concatenate([bd(at[pq]), bd(w[pq])], axis=1)) for pq in pairs}
    ahf = {pq: au[pq][:, :PAIR] for pq in pairs}
    uvf = {pq: au[pq][:, PAIR:] for pq in pairs}
    for z, c, pr in pairs:
        pq = (z, c, pr)
        rhat_ref[z, 0, rows(c), lsl[pr]] = (
            rt[pq] + fdot(bot[pq][:, :PAIR], bd(ahf[pq]))).astype(rhat_ref.dtype)
        yloc_ref[z, 0, rows(c), lsl[pr]] = fdot(
            bot[pq], cat0([bd(uvf[pq]), bdv[pq]])).astype(yloc_ref.dtype)
    for z, c, pr in pairs:
        pq = (z, c, pr)
        mm_ref[z, 0, c * npair + pr] = (
            jnp.where(same_head, eye_pair + _bdot_tn(ahf[pq], bk[pq][:C]), 0.0) * gam[pq]
        ).astype(mm_ref.dtype)
        nn_ref[z, 0, c * npair + pr] = jnp.where(
            same_head, _bdot_tn(cat0([uvf[pq].astype(BF16), vv[pq]]), bk[pq]), 0.0) * gam[pq]


def _mixer_kernel(lb_ref, stack_ref, *refs):
    nin = 7
    per_dir = [refs[z * nin:(z + 1) * nin] for z in range(2)]
    o_refs = [refs[2 * nin], refs[2 * nin + 2]]
    y_refs = [refs[2 * nin + 1], refs[2 * nin + 3]]
    st_ref, s_ref = refs[2 * nin + 4:]
    s = pl.program_id(1)
    C = CHUNK
    nchunk = TILE // C
    npair = RW_HEADS // 2
    dirs = (0, 1)
    lsl = [slice(pr * PAIR, (pr + 1) * PAIR) for pr in range(npair)]

    @pl.when(s == 0)
    def _():
        st_ref[...] = jnp.zeros_like(st_ref)
        s_ref[...] = jnp.zeros_like(s_ref)

    hgrn = [_hgrn_steps(*per_dir[z][:3], lb_ref, stack_ref, o_refs[z], st_ref, z) for z in dirs]

    def fill(n):
        for _ in range(n):
            for gen in hgrn:
                next(gen, None)

    rhat_ref, yloc_ref, mm_ref, nn_ref = zip(*[per_dir[z][3:] for z in dirs])
    st = {(z, pr): s_ref[z, pr] for z in dirs for pr in range(npair)}
    for u in range(nchunk):
        fill(HG_FILL)
        stb = {k: x.astype(BF16) for k, x in st.items()}
        for z in dirs:
            c = u if z == 0 else nchunk - 1 - u
            for pr in range(npair):
                y_refs[z][0, c * C:(c + 1) * C, lsl[pr]] = (
                    _bdot_nt(rhat_ref[z][0, 0, c * C:(c + 1) * C, lsl[pr]], stb[z, pr])
                    + yloc_ref[z][0, 0, c * C:(c + 1) * C, lsl[pr]].astype(F32)
                ).astype(y_refs[z].dtype)
            for pr in range(npair):
                st[z, pr] = (_bdot(stb[z, pr], mm_ref[z][0, 0, c * npair + pr])
                             + nn_ref[z][0, 0, c * npair + pr])
    for (z, pr), x in st.items():
        s_ref[z, pr] = x
    for gen in hgrn:
        for _ in gen:
            pass


def _rwkv_pre(p_rw, mu, vecs, dvecs, w2cat, a2cat, g2, hones, tri, nctx):
    B, TC, _ = p_rw.shape
    nt = TC // TILE
    hb = TILE // GRID_W
    nhb = TC // GRID_W
    ntr = (TILE // CHUNK) * (RW_HEADS // 2)
    full = lambda a: pl.BlockSpec(a.shape, lambda b, t: (0,) * a.ndim)
    both = pl.BlockSpec((2, 1, TILE, RW_WIDTH), lambda b, t: (0, b, t, 0))
    trans = pl.BlockSpec((2, 1, ntr, PAIR, PAIR), lambda b, t: (0, b, t, 0, 0))
    one = pl.BlockSpec((1, TILE, RW_WIDTH), lambda b, t: (b, t, 0))
    dir_bf = jax.ShapeDtypeStruct((2, B, TC, RW_WIDTH), BF16)
    return pl.pallas_call(
        functools.partial(_rwkv_pre_kernel, nt=nt, nctx=nctx),
        grid=(B, nt),
        in_specs=[pl.BlockSpec((1, TILE, RW_COLS), lambda b, t: (b, t, 0)),
                  pl.BlockSpec((1, GRID_W, RW_COLS), lambda b, t: (b, jnp.maximum(t * hb - 1, 0), 0)),
                  pl.BlockSpec((1, GRID_W, RW_COLS),
                               lambda b, t: (b, jnp.minimum((t + 1) * hb, nhb - 1), 0)),
                  full(mu), full(vecs), full(dvecs), full(w2cat), full(a2cat), full(g2),
                  full(hones), full(tri)],
        out_specs=[both, both, trans, trans, one, one],
        out_shape=[dir_bf, dir_bf,
                   jax.ShapeDtypeStruct((2, B, nt * ntr, PAIR, PAIR), BF16),
                   jax.ShapeDtypeStruct((2, B, nt * ntr, PAIR, PAIR), F32),
                   jax.ShapeDtypeStruct((B, TC, RW_WIDTH), BF16),
                   jax.ShapeDtypeStruct((B, TC, RW_WIDTH), BF16)],
        compiler_params=_params("arbitrary", "arbitrary"),
        name="rwkv7_pre",
    )(p_rw, p_rw, p_rw, mu, vecs, dvecs, w2cat, a2cat, g2, hones, tri)


def _mixer(p_hg, lbtab, stack, rhat, yloc, mm, nn, nctx):
    _, B, TC, _ = rhat.shape
    nt = TC // TILE
    ntr = (TILE // CHUNK) * (RW_HEADS // 2)
    assert HG_WIDTH == RW_WIDTH
    full = lambda a: pl.BlockSpec(a.shape, lambda b, s: (0,) * a.ndim)
    in_specs, operands = [full(lbtab), full(stack)], [lbtab, stack]
    out_specs = []
    for z in range(2):
        tile = functools.partial(_scan_tile, z, nt=nt, nctx=nctx)
        col = lambda c, tile=tile: pl.BlockSpec((1, TILE, HG_WIDTH), lambda b, s: (b, tile(s), c))
        dir_tile = pl.BlockSpec((1, 1, TILE, RW_WIDTH), lambda b, s, z=z, tile=tile: (z, b, tile(s), 0))
        trans = pl.BlockSpec((1, 1, ntr, PAIR, PAIR), lambda b, s, z=z, tile=tile: (z, b, tile(s), 0, 0))
        in_specs += [col(0), col(1 + z), col(3), dir_tile, dir_tile, trans, trans]
        operands += [p_hg, p_hg, p_hg, rhat, yloc, mm, nn]
        out_specs += [col(0), col(0)]
    out_shape = jax.ShapeDtypeStruct((B, TC, RW_WIDTH), BF16)
    return pl.pallas_call(
        _mixer_kernel,
        grid=(B, nt),
        in_specs=in_specs,
        out_specs=out_specs,
        out_shape=[out_shape] * 4,
        scratch_shapes=[pltpu.VMEM((2, HG_HEADS, HG_HEAD, HG_HEAD), F32),
                        pltpu.VMEM((2, RW_HEADS // 2, PAIR, PAIR), F32)],
        compiler_params=_params("arbitrary", "arbitrary"),
        name="mixer_scans",
    )(*operands)


def _merge_kernel(of_ref, ob_ref, og_ref, yf_ref, yb_ref, bon_ref, g_ref, gt_ref, x_ref, mod_ref,
                  hgn_ref, gn_ref, ng_ref, hones_ref, pa_ref, pb_ref, wo_ref, g23_ref, w1_ref, w2_ref,
                  o_ref):
    D = x_ref.shape[-1]
    H = HG_HEAD
    oh = of_ref[0].astype(F32) + ob_ref[0].astype(F32)
    og = og_ref[0]
    hgn = hgn_ref[...]
    parts = []
    for h in range(HG_HEADS):
        ls = slice(h * H, (h + 1) * H)
        parts.append(_rms(oh[:, ls], hgn) * (og[:, ls] * _sigmoid(og[:, ls])))
    o_hg = jnp.concatenate(parts, axis=-1)
    hones = hones_ref[...]
    y = yf_ref[0].astype(F32) + yb_ref[0].astype(F32)
    mean = _dot_exact_rhs(y, hones) * (1.0 / RW_HEAD)
    yc = y - mean
    var = _dot_exact_rhs(yc * yc, hones) * (1.0 / RW_HEAD)
    yn = yc * lax.rsqrt(var + RW_GN_EPS) * gn_ref[0:1, :] + gn_ref[1:2, :]
    o_rw = (yn + bon_ref[0].astype(F32)) * g_ref[0].astype(F32)
    gt = gt_ref[0].astype(F32)
    m = (_sigmoid(gt[:, :D]) * _bdot(o_hg, pa_ref[...])
         + _sigmoid(gt[:, D:]) * _bdot(o_rw, pb_ref[...]))
    yx = _bdot(m, wo_ref[...])
    gate = mod_ref[0, :, 2 * D:3 * D]
    x = x_ref[0] + gate * _rms(yx, ng_ref[...])
    F = w2_ref.shape[0]
    h = _rms(x, g23_ref[0:1, :]) * (1.0 + mod_ref[0, :, 4 * D:5 * D]) + mod_ref[0, :, 3 * D:4 * D]
    hb = h.astype(BF16)
    gtf = jnp.dot(hb, w1_ref[:, 0:F], preferred_element_type=F32)
    up = jnp.dot(hb, w1_ref[:, F:2 * F], preferred_element_type=F32)
    act = (gtf * _sigmoid(gtf) * up).astype(BF16)
    yf = jnp.dot(act, w2_ref[...], preferred_element_type=F32)
    o_ref[0] = x + mod_ref[0, :, 5 * D:6 * D] * _rms(yf, g23_ref[1:2, :])


def _merge(o_f, o_b, p_hg, y_f, y_b, bon, g, p_gt, xcat, modtab, hgn, gn, ng, hones, pa, pb, wo,
           g23, w1, w2, t0):
    B, TC, D = xcat.shape
    nt = TC // TILE - t0
    full = lambda a: pl.BlockSpec(a.shape, lambda b, t: (0,) * a.ndim)
    once = lambda a: pl.BlockSpec(a.shape, lambda b, t: (0,) * a.ndim, pipeline_mode=pl.Buffered(1))
    tok = pl.BlockSpec((1, TILE, RW_WIDTH), lambda b, t: (b, t + t0, 0))
    return pl.pallas_call(
        _merge_kernel,
        grid=(B, nt),
        in_specs=[tok, tok,
                  pl.BlockSpec((1, TILE, HG_WIDTH), lambda b, t: (b, t + t0, 4)),
                  tok, tok, tok, tok,
                  pl.BlockSpec((1, TILE, 2 * D), lambda b, t: (b, t + t0, 0)),
                  pl.BlockSpec((1, TILE, D), lambda b, t: (b, t + t0, 0)),
                  pl.BlockSpec((1, 1, modtab.shape[-1]),
                               lambda b, t: (2 * b + jnp.minimum(t + t0, 1), 0, 0)),
                  full(hgn), full(gn), full(ng), full(hones), once(pa), once(pb), once(wo),
                  full(g23), once(w1), once(w2)],
        out_specs=pl.BlockSpec((1, TILE, D), lambda b, t: (b, t, 0)),
        out_shape=jax.ShapeDtypeStruct((B, nt * TILE, D), F32),
        compiler_params=_params("arbitrary", "arbitrary"),
        name="merge_ffn",
    )(o_f, o_b, p_hg, y_f, y_b, bon, g, p_gt, xcat, modtab, hgn, gn, ng, hones, pa, pb, wo,
      g23, w1, w2)


def _ffn_kernel(x_ref, mod_ref, g_ref, w1_ref, w2_ref, o_ref):
    D = x_ref.shape[-1]
    F = w2_ref.shape[0]
    x = x_ref[0]
    h = _rms(x, g_ref[0:1, :]) * (1.0 + mod_ref[0, :, 4 * D:5 * D]) + mod_ref[0, :, 3 * D:4 * D]
    hb = h.astype(BF16)
    gt = jnp.dot(hb, w1_ref[:, 0:F], preferred_element_type=F32)
    up = jnp.dot(hb, w1_ref[:, F:2 * F], preferred_element_type=F32)
    act = (gt * _sigmoid(gt) * up).astype(BF16)
    y = jnp.dot(act, w2_ref[...], preferred_element_type=F32)
    o_ref[0] = x + mod_ref[0, :, 5 * D:6 * D] * _rms(y, g_ref[1:2, :])


def _ffn(x, modtab, g, w1, w2, t0):
    B, TX, D = x.shape
    nt = TX // TILE
    full = lambda a: pl.BlockSpec(a.shape, lambda b, t: (0,) * a.ndim)
    return pl.pallas_call(
        _ffn_kernel,
        grid=(B, nt),
        in_specs=[pl.BlockSpec((1, TILE, D), lambda b, t: (b, t, 0)),
                  pl.BlockSpec((1, 1, modtab.shape[-1]),
                               lambda b, t: (2 * b + jnp.minimum(t + t0, 1), 0, 0)),
                  full(g), full(w1), full(w2)],
        out_specs=pl.BlockSpec((1, TILE, D), lambda b, t: (b, t, 0)),
        out_shape=jax.ShapeDtypeStruct((B, TX, D), F32),
        compiler_params=_params("arbitrary", "arbitrary"),
        name="swiglu_ffn",
    )(x, modtab, g, w1, w2)


def kernel(x, c, ctx, c_ctx, ada_w, ada_b, norm_g, w_in, hg_lb_logits, hg_norm_g, rw_mu, rw_w0, rw_w2,
           rw_a0, rw_a2, rw_g2, rw_kk, rw_ka, rw_rk, rw_gn_g, rw_gn_b, proj_a, proj_b, w_out, ffn_w1,
           ffn_w2):
    B, T, D = x.shape
    LCTX = ctx.shape[1]
    L = ada_w.shape[0]
    assert T % TILE == 0 and LCTX == TILE and TILE % GRID_W == 0 and T // TILE >= 2
    nctx = LCTX // TILE

    rows = -(-(B + 1) // 8) * 8
    cond = jnp.zeros((rows, D), F32).at[:B].set(c).at[B].set(c_ctx)
    mods = _mods(cond, ada_w, ada_b)

    lb_cum = jnp.cumsum(jax.nn.softmax(hg_lb_logits.astype(F32), axis=0), axis=0)
    hg_lb = lb_cum - lb_cum[:1]
    lbtab = jnp.stack([jnp.log(hg_lb), jnp.log1p(-hg_lb), 1.0 - hg_lb], axis=2)

    stack = jnp.asarray(_hgrn_stack(), BF16)
    tri = jnp.asarray(_rw_tri(), BF16)
    hones = jnp.asarray(_head_ones(), BF16)

    zeros = jnp.zeros_like(rw_w2[:, 0])
    w2cat = jnp.stack([jnp.concatenate([rw_w2[:, 0], zeros], axis=1),
                       jnp.concatenate([zeros, rw_w2[:, 1]], axis=1)], axis=1).astype(BF16)
    a2cat = jnp.stack([jnp.concatenate([rw_a2[:, 0], zeros], axis=1),
                       jnp.concatenate([zeros, rw_a2[:, 1]], axis=1)], axis=1).astype(BF16)

    xcat = jnp.concatenate([ctx, x], axis=1)
    for l in range(L):
        last = l == L - 1
        lat = mods[l, :B]
        ctxm = jnp.broadcast_to(mods[l, B], lat.shape)
        modtab = jnp.stack([ctxm, lat], axis=1).reshape(2 * B, 1, 6 * D)
        wl = w_in[l].astype(BF16)
        p_hg, p_rw, p_gt = _inproj(xcat, modtab, norm_g[l, 0:1], wl[:, :HG_COLS],
                                   wl[:, HG_COLS:HG_COLS + RW_COLS], wl[:, HG_COLS + RW_COLS:])
        vecs = jnp.stack([rw_kk[l], rw_ka[l], rw_rk[l].reshape(-1)], axis=0)
        dvecs = jnp.stack([rw_w0[l], rw_a0[l]], axis=1)
        rhat, yloc, mm, nn, bon, g = _rwkv_pre(p_rw, rw_mu[l][None, :], vecs, dvecs, w2cat[l],
                                               a2cat[l], rw_g2[l].astype(BF16), hones, tri, nctx)
        o_f, y_f, o_b, y_b = _mixer(p_hg, lbtab[l], stack, rhat, yloc, mm, nn, nctx)
        t0 = nctx if last else 0
        xcat = _merge(o_f, o_b, p_hg, y_f, y_b, bon, g, p_gt, xcat, modtab, hg_norm_g[l][None, :],
                      jnp.stack([rw_gn_g[l], rw_gn_b[l]], axis=0), norm_g[l, 1:2], hones,
                      proj_a[l].astype(BF16), proj_b[l].astype(BF16), w_out[l].astype(BF16),
                      norm_g[l, 2:4], ffn_w1[l].astype(BF16), ffn_w2[l].astype(BF16), t0)
    return xcat
```
